```python
import jax, jax.numpy as jnp
from jax import lax
import numpy as np

D_MODEL = 1024
BATCH = 2
SEQ = 16384
DEPTH = 4

GRID_W = 64
CTX_LEN = 256
HEAD_DIM = 64
N_Q_HEADS = D_MODEL // HEAD_DIM
N_KV_HEADS = N_Q_HEADS // 4
GQA_GROUP = N_Q_HEADS // N_KV_HEADS
QKV_WIDTH = (N_Q_HEADS + 2 * N_KV_HEADS) * HEAD_DIM
ROPE_AXIS_DIM = HEAD_DIM // 2
ROPE_THETA = 10000.0
Q_BLOCK = 128
WINDOW = 128
N_EXPERTS = 16
EC_CAPACITY = 2
D_EXPERT = 2 * D_MODEL
N_MOD = 6
N_GLOBAL_LAYERS = (DEPTH + 1) // 2
N_WINDOW_LAYERS = DEPTH // 2
NORM_EPS = 1e-6
NEG_INF = -1e30

kernel_name = "hybrid_interleaved_gqa_window_ec_moe_dit"


def rms_norm(x, g):
    xf = x.astype(jnp.float32)
    y = xf * lax.rsqrt(jnp.mean(xf * xf, axis=-1, keepdims=True) + NORM_EPS)
    return (y * g.astype(jnp.float32)).astype(x.dtype)


def modulate(h, shift, scale):
    return h * (1 + scale) + shift


def axial_rope_tables(n_tokens):
    rows = n_tokens // GRID_W
    row = jnp.repeat(jnp.arange(rows, dtype=jnp.float32), GRID_W)
    col = jnp.tile(jnp.arange(GRID_W, dtype=jnp.float32), rows)
    inv_freq = 1.0 / (ROPE_THETA ** (jnp.arange(0, ROPE_AXIS_DIM, 2, dtype=jnp.float32) / ROPE_AXIS_DIM))
    ang = jnp.stack([row[:, None] * inv_freq, col[:, None] * inv_freq], axis=1)
    return jnp.cos(ang), jnp.sin(ang)


def apply_axial_rope(x, cos, sin):
    half = ROPE_AXIS_DIM // 2
    L = x.shape[1]
    xs = x.reshape(x.shape[:-1] + (2, 2, half))
    x1, x2 = xs[..., 0, :], xs[..., 1, :]
    bshape = (1, L) + (1,) * (x.ndim - 3) + (2, half)
    cs = cos.reshape(bshape).astype(x.dtype)
    sn = sin.reshape(bshape).astype(x.dtype)
    out = jnp.stack([x1 * cs - x2 * sn, x2 * cs + x1 * sn], axis=-2)
    return out.reshape(x.shape)


def project_qkv(h, w_qkv):
    B, L, _ = h.shape
    qkv = h @ w_qkv
    nq = N_Q_HEADS * HEAD_DIM
    nk = N_KV_HEADS * HEAD_DIM
    q = qkv[..., :nq].reshape(B, L, N_KV_HEADS, GQA_GROUP, HEAD_DIM)
    k = qkv[..., nq:nq + nk].reshape(B, L, N_KV_HEADS, HEAD_DIM)
    v = qkv[..., nq + nk:].reshape(B, L, N_KV_HEADS, HEAD_DIM)
    return q, k, v


def attend(q, k, v, mask=None, sink=None):
    s = jnp.einsum('bqhgd,bkhd->bhgqk', q, k).astype(jnp.float32) * (HEAD_DIM ** -0.5)
    if mask is not None:
        s = jnp.where(mask, s, NEG_INF)
    if sink is not None:
        sk = jnp.broadcast_to(sink.astype(jnp.float32)[None, :, :, None, None], s.shape[:-1] + (1,))
        p = jax.nn.softmax(jnp.concatenate([s, sk], axis=-1), axis=-1)[..., :-1]
    else:
        p = jax.nn.softmax(s, axis=-1)
    return jnp.einsum('bhgqk,bkhd->bqhgd', p.astype(v.dtype), v)


def merge_heads(o, w_o):
    B, L = o.shape[:2]
    return o.reshape(B, L, D_MODEL) @ w_o


def global_attention_mixer(hx, hc, w_qkv, w_o, q_g, k_g, cos, sin, with_ctx):
    q_l, k_l, v_l = project_qkv(hx, w_qkv)
    q_c, k_c, v_c = project_qkv(hc, w_qkv)
    q_l = apply_axial_rope(rms_norm(q_l, q_g), cos, sin)
    k_l = apply_axial_rope(rms_norm(k_l, k_g), cos, sin)
    q_c = rms_norm(q_c, q_g)
    k_c = rms_norm(k_c, k_g)
    keys = jnp.concatenate([k_c, k_l], axis=1)
    vals = jnp.concatenate([v_c, v_l], axis=1)
    B, L = hx.shape[:2]
    nb = L // Q_BLOCK
    qb = q_l.reshape(B, nb, Q_BLOCK, N_KV_HEADS, GQA_GROUP, HEAD_DIM).swapaxes(0, 1)
    ob = lax.map(lambda qblk: attend(qblk, keys, vals), qb)
    o_l = merge_heads(ob.swapaxes(0, 1).reshape(B, L, N_KV_HEADS, GQA_GROUP, HEAD_DIM), w_o)
    o_c = merge_heads(attend(q_c, k_c, v_c), w_o) if with_ctx else None
    return o_l, o_c


def window_attention_mixer(hx, hc, w_qkv, w_o, sink_h, cos, sin, with_ctx):
    q_l, k_l, v_l = project_qkv(hx, w_qkv)
    q_c, k_c, v_c = project_qkv(hc, w_qkv)
    q_l = apply_axial_rope(q_l, cos, sin)
    k_l = apply_axial_rope(k_l, cos, sin)
    sink = sink_h.reshape(N_KV_HEADS, GQA_GROUP)
    B, L = hx.shape[:2]
    Lc = hc.shape[1]
    nb = L // Q_BLOCK
    span = Q_BLOCK + 2 * WINDOW
    pad = ((0, 0), (WINDOW, WINDOW), (0, 0), (0, 0))
    k_pad = jnp.pad(k_l, pad)
    v_pad = jnp.pad(v_l, pad)
    a = jnp.arange(Q_BLOCK)
    b = jnp.arange(span)
    band = (b[None, :] >= a[:, None]) & (b[None, :] <= a[:, None] + 2 * WINDOW)
    ctx_ok = jnp.ones((Q_BLOCK, Lc), dtype=bool)
    qb = q_l.reshape(B, nb, Q_BLOCK, N_KV_HEADS, GQA_GROUP, HEAD_DIM).swapaxes(0, 1)

    def block(args):
        qblk, n = args
        start = n * Q_BLOCK
        ks = lax.dynamic_slice_in_dim(k_pad, start, span, axis=1)
        vs = lax.dynamic_slice_in_dim(v_pad, start, span, axis=1)
        j = start - WINDOW + b
        inb = (j >= 0) & (j < L)
        mask = jnp.concatenate([ctx_ok, band & inb[None, :]], axis=1)
        keys = jnp.concatenate([k_c, ks], axis=1)
        vals = jnp.concatenate([v_c, vs], axis=1)
        return attend(qblk, keys, vals, mask, sink)

    ob = lax.map(block, (qb, jnp.arange(nb)))
    o_l = merge_heads(ob.swapaxes(0, 1).reshape(B, L, N_KV_HEADS, GQA_GROUP, HEAD_DIM), w_o)
    o_c = merge_heads(attend(q_c, k_c, v_c, None, sink), w_o) if with_ctx else None
    return o_l, o_c


def expert_choice_ffn(h, w_router, w_gate, w_up, w_down):
    B, L, D = h.shape
    cap = max(1, EC_CAPACITY * L // N_EXPERTS)
    aff = jax.nn.softmax(jnp.einsum('bld,de->ble', h, w_router).astype(jnp.float32), axis=-1)
    gates, idx = lax.top_k(aff.transpose(0, 2, 1), cap)
    x_sel = jax.vmap(lambda hb, ib: hb[ib])(h, idx)
    g = jnp.einsum('becd,edf->becf', x_sel, w_gate)
    u = jnp.einsum('becd,edf->becf', x_sel, w_up)
    y = jnp.einsum('becf,efd->becd', jax.nn.silu(g) * u, w_down)
    y = y * gates[..., None].astype(y.dtype)
    return jax.vmap(lambda yb, ib: jnp.zeros((L, D), yb.dtype).at[ib.reshape(-1)].add(yb.reshape(-1, D)))(y, idx)


def setup_inputs(seed: int = 0) -> dict:
    key = jax.random.key(seed)
    ks = jax.random.split(key, 20)
    f32 = jnp.float32
    nrm = lambda k, shape, s: jax.random.normal(k, shape, f32) * s
    return {
        "x": nrm(ks[0], (BATCH, SEQ, D_MODEL), 1.0),
        "c": nrm(ks[1], (BATCH, D_MODEL), 1.0),
        "ctx": nrm(ks[2], (BATCH, CTX_LEN, D_MODEL), 1.0),
        "c_ctx": nrm(ks[3], (D_MODEL,), 1.0),
        "w_mod": nrm(ks[4], (DEPTH, D_MODEL, N_MOD * D_MODEL), 0.5 * D_MODEL ** -0.5),
        "b_mod": nrm(ks[5], (DEPTH, N_MOD * D_MODEL), 0.01),
        "norm_mix_g": 1.0 + nrm(ks[6], (DEPTH, D_MODEL), 0.02),
        "norm_ffn_g": 1.0 + nrm(ks[7], (DEPTH, D_MODEL), 0.02),
        "w_qkv": nrm(ks[8], (DEPTH, D_MODEL, QKV_WIDTH), D_MODEL ** -0.5),
        "w_o": nrm(ks[9], (DEPTH, D_MODEL, D_MODEL), D_MODEL ** -0.5),
        "q_norm_g": 1.0 + nrm(ks[10], (N_GLOBAL_LAYERS, HEAD_DIM), 0.02),
        "k_norm_g": 1.0 + nrm(ks[11], (N_GLOBAL_LAYERS, HEAD_DIM), 0.02),
        "attn_sink": nrm(ks[12], (N_WINDOW_LAYERS, N_Q_HEADS), 0.5),
        "w_router": nrm(ks[13], (DEPTH, D_MODEL, N_EXPERTS), D_MODEL ** -0.5),
        "w_gate": nrm(ks[14], (DEPTH, N_EXPERTS, D_MODEL, D_EXPERT), D_MODEL ** -0.5),
        "w_up": nrm(ks[15], (DEPTH, N_EXPERTS, D_MODEL, D_EXPERT), D_MODEL ** -0.5),
        "w_down": nrm(ks[16], (DEPTH, N_EXPERTS, D_EXPERT, D_MODEL), D_EXPERT ** -0.5),
        "final_norm_g": 1.0 + nrm(ks[17], (D_MODEL,), 0.02),
    }


def reference(x, c, ctx, c_ctx, w_mod, b_mod, norm_mix_g, norm_ffn_g, w_qkv, w_o, q_norm_g, k_norm_g,
              attn_sink, w_router, w_gate, w_up, w_down, final_norm_g):
    L = x.shape[1]
    cos, sin = axial_rope_tables(L)
    silu_c = jax.nn.silu(c)
    silu_cc = jax.nn.silu(c_ctx)
    for i in range(DEPTH):
        last = i == DEPTH - 1
        mod_x = silu_c @ w_mod[i] + b_mod[i]
        sh1, sc1, g1, sh2, sc2, g2 = [m[:, None, :] for m in jnp.split(mod_x, N_MOD, axis=-1)]
        mod_c = silu_cc @ w_mod[i] + b_mod[i]
        csh1, csc1, cg1, csh2, csc2, cg2 = jnp.split(mod_c, N_MOD, axis=-1)

        hx = modulate(rms_norm(x, norm_mix_g[i]), sh1, sc1)
        hc = modulate(rms_norm(ctx, norm_mix_g[i]), csh1, csc1)
        if i % 2 == 0:
            j = i // 2
            o_x, o_c = global_attention_mixer(hx, hc, w_qkv[i], w_o[i], q_norm_g[j], k_norm_g[j],
                                              cos, sin, not last)
        else:
            j = i // 2
            o_x, o_c = window_attention_mixer(hx, hc, w_qkv[i], w_o[i], attn_sink[j], cos, sin, not last)
        x = x + g1 * o_x
        if not last:
            ctx = ctx + cg1 * o_c

        hx = modulate(rms_norm(x, norm_ffn_g[i]), sh2, sc2)
        x = x + g2 * expert_choice_ffn(hx, w_router[i], w_gate[i], w_up[i], w_down[i])
        if not last:
            hc = modulate(rms_norm(ctx, norm_ffn_g[i]), csh2, csc2)
            ctx = ctx + cg2 * expert_choice_ffn(hc, w_router[i], w_gate[i], w_up[i], w_down[i])
    return rms_norm(x, final_norm_g)
```

```python
import functools

import jax
import jax.numpy as jnp
from jax import lax
from jax.experimental import pallas as pl
from jax.experimental.pallas import tpu as pltpu

HEAD_DIM = 64
N_KV_HEADS = 4
GQA_GROUP = 4
N_Q_HEADS = N_KV_HEADS * GQA_GROUP
ROPE_AXIS_DIM = HEAD_DIM // 2
ROPE_HALF = ROPE_AXIS_DIM // 2
ROPE_THETA = 10000.0
GRID_W = 64
WINDOW = 128
N_EXPERTS = 16
EC_CAPACITY = 2
N_MOD = 6
NORM_EPS = 1e-6
NEG_BIG = -1e30

TOKEN_TILE = 256
LANES = 128
V_ROWS = 80
ROW_WINDOW = 128
VMEM_LIMIT = 56 * 1024 * 1024

F32 = jnp.float32
BF16 = jnp.bfloat16


def _cparams(sem, vmem=None):
    return pltpu.CompilerParams(dimension_semantics=sem, vmem_limit_bytes=vmem)


def _mod_kernel(c_ref, w_ref, b_ref, o_ref):
    c = c_ref[...]
    s = c * (1.0 / (1.0 + jnp.exp(-c)))
    o_ref[...] = jnp.dot(s, w_ref[...], preferred_element_type=F32,
                         precision=lax.Precision.HIGHEST) + b_ref[...]


def _modulation(cond, w_mod, b_mod):
    depth, d, n = w_mod.shape
    nb = n // 4
    return pl.pallas_call(
        _mod_kernel,
        out_shape=jax.ShapeDtypeStruct((depth, 8, n), F32),
        grid=(depth, n // nb),
        in_specs=[pl.BlockSpec((8, d), lambda l, j: (0, 0)),
                  pl.BlockSpec((None, d, nb), lambda l, j: (l, 0, j)),
                  pl.BlockSpec((None, 1, nb), lambda l, j: (l, 0, j))],
        out_specs=pl.BlockSpec((None, 8, nb), lambda l, j: (l, 0, j)),
        compiler_params=_cparams(("arbitrary", "arbitrary"), VMEM_LIMIT),
        name="modulation",
    )(cond, w_mod, b_mod.reshape(depth, 1, n))


def _pre_attn_kernel(x_ref, mod_ref, g_ref, wT_ref, ct_ref, st_ref, gqk_ref,
                     q_ref, k_ref, v_ref, *, qk_norm):
    tt = x_ref.shape[0]
    x = x_ref[...]
    ms = jnp.mean(x * x, axis=-1, keepdims=True)
    h = x * lax.rsqrt(ms + NORM_EPS) * g_ref[...]
    h = h * (1.0 + mod_ref[1:2, :]) + mod_ref[0:1, :]
    qkv = lax.dot_general(wT_ref[...], h.astype(BF16), (((1,), (1,)), ((), ())),
                          preferred_element_type=F32)
    n_qk = (N_Q_HEADS + N_KV_HEADS) * HEAD_DIM
    n_heads = N_Q_HEADS + N_KV_HEADS
    qk = qkv[:n_qk].reshape(n_heads, HEAD_DIM, tt)
    if qk_norm:
        ss = jnp.mean(qk * qk, axis=1, keepdims=True)
        qk = qk * lax.rsqrt(ss + NORM_EPS) * gqk_ref[...].reshape(n_heads, HEAD_DIM, tt)
    q5 = qk.reshape(n_heads, 2, 2, ROPE_HALF, tt)
    swapped = jnp.concatenate([q5[:, :, 1:2], q5[:, :, 0:1]], axis=2).reshape(n_heads, HEAD_DIM, tt)
    qk = qk * ct_ref[...][None] + swapped * st_ref[...][None]

    zeros64 = jnp.zeros((HEAD_DIM, tt), BF16)
    for hh in range(N_KV_HEADS):
        for g in range(GQA_GROUP):
            piece = (qk[hh * GQA_GROUP + g] * (HEAD_DIM ** -0.5)).astype(BF16)
            pair = [piece, zeros64] if hh % 2 == 0 else [zeros64, piece]
            q_ref[hh, :, g * tt:(g + 1) * tt] = jnp.concatenate(pair, axis=0)
    kT = qk[N_Q_HEADS:].reshape(N_KV_HEADS * HEAD_DIM, tt)
    for p in range(N_KV_HEADS // 2):
        k_ref[p] = kT[p * LANES:(p + 1) * LANES].T.astype(BF16)
    v = qkv[n_qk:]
    row = lax.broadcasted_iota(jnp.int32, (V_ROWS - HEAD_DIM, tt), 0)
    tail = jnp.where(row == 0, 1.0, 0.0).astype(BF16)
    for hh in range(N_KV_HEADS):
        v_ref[hh] = jnp.concatenate([v[hh * HEAD_DIM:(hh + 1) * HEAD_DIM].astype(BF16), tail], axis=0)


def _pre_attn(xs, modsel, g, wT, ct, st, gqk, qk_norm):
    b, lt, d = xs.shape
    tt = TOKEN_TILE
    nt = lt // tt
    n_qkv = wT.shape[0]
    n_qk = gqk.shape[0]
    out_shape = (
        jax.ShapeDtypeStruct((b, N_KV_HEADS, nt, 2 * HEAD_DIM, GQA_GROUP * tt), BF16),
        jax.ShapeDtypeStruct((b, N_KV_HEADS // 2, lt, 2 * HEAD_DIM), BF16),
        jax.ShapeDtypeStruct((b, N_KV_HEADS, V_ROWS, lt), BF16),
    )
    return pl.pallas_call(
        functools.partial(_pre_attn_kernel, qk_norm=qk_norm),
        out_shape=out_shape,
        grid=(b, nt),
        in_specs=[
            pl.BlockSpec((None, tt, d), lambda i, t: (i, t, 0)),
            pl.BlockSpec((None, None, N_MOD, d), lambda i, t: (i, jnp.where(t == nt - 1, 0, 1), 0, 0)),
            pl.BlockSpec((1, d), lambda i, t: (0, 0)),
            pl.BlockSpec((n_qkv, d), lambda i, t: (0, 0)),
            pl.BlockSpec((HEAD_DIM, tt), lambda i, t: (0, t)),
            pl.BlockSpec((HEAD_DIM, tt), lambda i, t: (0, t)),
            pl.BlockSpec((n_qk, tt), lambda i, t: (0, 0)),
        ],
        out_specs=(
            pl.BlockSpec((None, N_KV_HEADS, None, 2 * HEAD_DIM, GQA_GROUP * tt), lambda i, t: (i, 0, t, 0, 0)),
            pl.BlockSpec((None, N_KV_HEADS // 2, tt, 2 * HEAD_DIM), lambda i, t: (i, 0, t, 0)),
            pl.BlockSpec((None, N_KV_HEADS, V_ROWS, tt), lambda i, t: (i, 0, 0, t)),
        ),
        compiler_params=_cparams(("parallel", "arbitrary"), VMEM_LIMIT),
        name="pre_attn",
    )(xs, modsel, g, wT, ct, st, gqk)


def _attn_global_kernel(q_ref, k_ref, v_ref, kc_ref, vc_ref, o_ref, m_ref, acc_ref, *, chunk):
    j = pl.program_id(3)
    n_tiles = q_ref.shape[0]
    tt = q_ref.shape[2] // GQA_GROUP
    tk = k_ref.shape[0]
    q = jnp.concatenate([q_ref[a] for a in range(n_tiles)], axis=1) if n_tiles > 1 else q_ref[0]

    def absorb(k, vT):
        s = jnp.dot(k, q, preferred_element_type=F32)
        m_old = m_ref[...]
        m_new = jnp.maximum(m_old, jnp.max(s, axis=0, keepdims=True))
        p = jnp.exp(s - m_new).astype(BF16)
        acc_ref[...] = acc_ref[...] * jnp.exp(m_old - m_new) + jnp.dot(vT, p, preferred_element_type=F32)
        m_ref[...] = m_new

    @pl.when(j == 0)
    def _():
        m_ref[...] = jnp.full(m_ref.shape, NEG_BIG, F32)
        acc_ref[...] = jnp.zeros(acc_ref.shape, F32)
        absorb(kc_ref[...], vc_ref[...])

    for c in range(tk // chunk):
        absorb(k_ref[c * chunk:(c + 1) * chunk, :], v_ref[:, c * chunk:(c + 1) * chunk])

    @pl.when(j == pl.num_programs(3) - 1)
    def _():
        acc = acc_ref[...]
        o = acc[:HEAD_DIM] / acc[HEAD_DIM:HEAD_DIM + 1]
        for a in range(n_tiles):
            for g in range(GQA_GROUP):
                c0 = (a * GQA_GROUP + g) * tt
                o_ref[g, :, a * tt:(a + 1) * tt] = o[:, c0:c0 + tt].astype(BF16)


def _attn_global(qT, k, vT, n_lat):
    b = qT.shape[0]
    tt = TOKEN_TILE
    n_lat_tiles = n_lat // tt
    q_tiles = 2 if n_lat_tiles % 2 == 0 else 1
    tk = 2048 if n_lat % 2048 == 0 else tt
    chunk = 512 if tk % 512 == 0 else tt
    tq = q_tiles * tt
    ctx_tile = n_lat // tt
    n_cols = q_tiles * GQA_GROUP * tt
    return pl.pallas_call(
        functools.partial(_attn_global_kernel, chunk=chunk),
        out_shape=jax.ShapeDtypeStruct((b, N_Q_HEADS, HEAD_DIM, n_lat), BF16),
        grid=(b, N_KV_HEADS, n_lat // tq, n_lat // tk),
        in_specs=[
            pl.BlockSpec((None, None, q_tiles, 2 * HEAD_DIM, GQA_GROUP * tt), lambda i, h, qi, kj: (i, h, qi, 0, 0)),
            pl.BlockSpec((None, None, tk, 2 * HEAD_DIM), lambda i, h, qi, kj: (i, h // 2, kj, 0)),
            pl.BlockSpec((None, None, V_ROWS, tk), lambda i, h, qi, kj: (i, h, 0, kj)),
            pl.BlockSpec((None, None, tt, 2 * HEAD_DIM), lambda i, h, qi, kj: (i, h // 2, ctx_tile, 0)),
            pl.BlockSpec((None, None, V_ROWS, tt), lambda i, h, qi, kj: (i, h, 0, ctx_tile)),
        ],
        out_specs=pl.BlockSpec((None, GQA_GROUP, HEAD_DIM, tq), lambda i, h, qi, kj: (i, h, 0, qi)),
        scratch_shapes=[pltpu.VMEM((1, n_cols), F32), pltpu.VMEM((V_ROWS, n_cols), F32)],
        compiler_params=_cparams(("parallel", "parallel", "parallel", "arbitrary"), VMEM_LIMIT),
        name="attn_global",
    )(qT, k, vT, k, vT)


def _attn_local_kernel(*refs, band, has_sink):
    refs = list(refs)
    q_ref = refs.pop(0)
    kc_ref, vc_ref = refs.pop(0), refs.pop(0)
    if band:
        kp_ref, k_ref, kn_ref, vp_ref, v_ref, vn_ref = [refs.pop(0) for _ in range(6)]
    if has_sink:
        sink_ref = refs.pop(0)
    o_ref = refs.pop(0)
    tt = q_ref.shape[1] // GQA_GROUP
    n_ctx = kc_ref.shape[0]
    q = q_ref[...]
    if band:
        k = jnp.concatenate([kc_ref[...], kp_ref[...], k_ref[...], kn_ref[...]], axis=0)
        vT = jnp.concatenate([vc_ref[...], vp_ref[...], v_ref[...], vn_ref[...]], axis=1)
    else:
        k, vT = kc_ref[...], vc_ref[...]
    s = jnp.dot(k, q, preferred_element_type=F32)
    if band:
        i = pl.program_id(2)
        kk = lax.broadcasted_iota(jnp.int32, s.shape, 0) - n_ctx
        t = lax.broadcasted_iota(jnp.int32, s.shape, 1) % tt
        lo = jnp.where(i == 0, WINDOW, 0)
        hi = jnp.where(i == pl.num_programs(2) - 1, WINDOW + tt, 2 * WINDOW + tt)
        ok = (kk < 0) | ((kk >= t) & (kk <= t + 2 * WINDOW) & (kk >= lo) & (kk < hi))
        s = jnp.where(ok, s, NEG_BIG)
    m = jnp.max(s, axis=0, keepdims=True)
    if has_sink:
        sink = sink_ref[...]
        m = jnp.maximum(m, sink)
    p = jnp.exp(s - m).astype(BF16)
    acc = jnp.dot(vT, p, preferred_element_type=F32)
    denom = acc[HEAD_DIM:HEAD_DIM + 1]
    if has_sink:
        denom = denom + jnp.exp(sink - m)
    o = acc[:HEAD_DIM] / denom
    for g in range(GQA_GROUP):
        o_ref[g] = o[:, g * tt:(g + 1) * tt].astype(BF16)


def _attn_local(qT, k, vT, sink, n_lat, band):
    b = qT.shape[0]
    tt = TOKEN_TILE
    ctx_tile = n_lat // tt
    n_steps = n_lat // tt if band else 1
    half = tt // WINDOW
    last_blk = n_lat // WINDOW - 1
    has_sink = sink is not None
    q_idx = (lambda i, h, t: (i, h, t, 0, 0)) if band else (lambda i, h, t: (i, h, ctx_tile, 0, 0))
    in_specs = [
        pl.BlockSpec((None, None, None, 2 * HEAD_DIM, GQA_GROUP * tt), q_idx),
        pl.BlockSpec((None, None, tt, 2 * HEAD_DIM), lambda i, h, t: (i, h // 2, ctx_tile, 0)),
        pl.BlockSpec((None, None, V_ROWS, tt), lambda i, h, t: (i, h, 0, ctx_tile)),
    ]
    args = [qT, k, vT]
    if band:
        prev = lambda t: jnp.maximum(t * half - 1, 0)
        nxt = lambda t: jnp.minimum((t + 1) * half, last_blk)
        in_specs += [
            pl.BlockSpec((None, None, WINDOW, 2 * HEAD_DIM), lambda i, h, t: (i, h // 2, prev(t), 0)),
            pl.BlockSpec((None, None, tt, 2 * HEAD_DIM), lambda i, h, t: (i, h // 2, t, 0)),
            pl.BlockSpec((None, None, WINDOW, 2 * HEAD_DIM), lambda i, h, t: (i, h // 2, nxt(t), 0)),
            pl.BlockSpec((None, None, V_ROWS, WINDOW), lambda i, h, t: (i, h, 0, prev(t))),
            pl.BlockSpec((None, None, V_ROWS, tt), lambda i, h, t: (i, h, 0, t)),
            pl.BlockSpec((None, None, V_ROWS, WINDOW), lambda i, h, t: (i, h, 0, nxt(t))),
        ]
        args += [k, k, k, vT, vT, vT]
    if has_sink:
        in_specs.append(pl.BlockSpec((None, 1, GQA_GROUP * tt), lambda i, h, t: (h, 0, 0)))
        args.append(sink)
    n_out = n_lat if band else tt
    return pl.pallas_call(
        functools.partial(_attn_local_kernel, band=band, has_sink=has_sink),
        out_shape=jax.ShapeDtypeStruct((b, N_Q_HEADS, HEAD_DIM, n_out), BF16),
        grid=(b, N_KV_HEADS, n_steps),
        in_specs=in_specs,
        out_specs=pl.BlockSpec((None, GQA_GROUP, HEAD_DIM, tt), lambda i, h, t: (i, h, 0, t)),
        compiler_params=_cparams(("parallel", "parallel", "arbitrary"), VMEM_LIMIT),
        name="attn_window" if band else "attn_ctx",
    )(*args)


def _post_attn_kernel(ol_ref, oc_ref, x_ref, mod_ref, g_ref, woT_ref, wr_ref,
                      x1_ref, h2_ref, aff_ref, affT_ref):
    t = pl.program_id(1)
    is_ctx = t == pl.num_programs(1) - 1
    oT = jnp.where(is_ctx, oc_ref[...], ol_ref[...])
    yT = jnp.dot(woT_ref[...], oT, preferred_element_type=F32)
    x1 = x_ref[...] + mod_ref[2:3, :] * yT.T
    x1_ref[...] = x1
    ms = jnp.mean(x1 * x1, axis=-1, keepdims=True)
    h2 = x1 * lax.rsqrt(ms + NORM_EPS) * g_ref[...]
    h2 = h2 * (1.0 + mod_ref[4:5, :]) + mod_ref[3:4, :]
    h2_ref[...] = h2.astype(BF16)
    logits = jnp.dot(h2, wr_ref[...], preferred_element_type=F32, precision=lax.Precision.HIGHEST)
    lane = lax.broadcasted_iota(jnp.int32, logits.shape, 1)
    logits = jnp.where(lane < N_EXPERTS, logits, NEG_BIG)
    e = jnp.exp(logits - jnp.max(logits, axis=-1, keepdims=True))
    aff = e / jnp.sum(e, axis=-1, keepdims=True)
    aff_ref[...] = aff
    affT_ref[...] = aff.T[:N_EXPERTS]


def _post_attn(o_lat, o_ctx, xs, modsel, g, woT, wr):
    b, lt, d = xs.shape
    tt = TOKEN_TILE
    nt = lt // tt
    n_lat_tiles = nt - 1
    out_shape = (
        jax.ShapeDtypeStruct((b, lt, d), F32),
        jax.ShapeDtypeStruct((b, lt, d), BF16),
        jax.ShapeDtypeStruct((b, lt, LANES), F32),
        jax.ShapeDtypeStruct((b, N_EXPERTS, lt), F32),
    )
    return pl.pallas_call(
        _post_attn_kernel,
        out_shape=out_shape,
        grid=(b, nt),
        in_specs=[
            pl.BlockSpec((None, d, tt), lambda i, t: (i, 0, jnp.minimum(t, n_lat_tiles - 1))),
            pl.BlockSpec((None, d, tt), lambda i, t: (i, 0, 0)),
            pl.BlockSpec((None, tt, d), lambda i, t: (i, t, 0)),
            pl.BlockSpec((None, None, N_MOD, d), lambda i, t: (i, jnp.where(t == nt - 1, 0, 1), 0, 0)),
            pl.BlockSpec((1, d), lambda i, t: (0, 0)),
            pl.BlockSpec((d, d), lambda i, t: (0, 0)),
            pl.BlockSpec((d, LANES), lambda i, t: (0, 0)),
        ],
        out_specs=(
            pl.BlockSpec((None, tt, d), lambda i, t: (i, t, 0)),
            pl.BlockSpec((None, tt, d), lambda i, t: (i, t, 0)),
            pl.BlockSpec((None, tt, LANES), lambda i, t: (i, t, 0)),
            pl.BlockSpec((None, N_EXPERTS, tt), lambda i, t: (i, 0, t)),
        ),
        compiler_params=_cparams(("parallel", "arbitrary"), VMEM_LIMIT),
        name="post_attn",
    )(o_lat, o_ctx, xs, modsel, g, woT, wr)


def _route_kernel(affT_ref, pos_ref, cnt_ref, *, n_lat, cap_lat, cap_ctx):
    tt = TOKEN_TILE
    lt = affT_ref.shape[1]
    ne = affT_ref.shape[0]
    r = lax.broadcasted_iota(jnp.int32, (tt, tt), 0)
    c = lax.broadcasted_iota(jnp.int32, (tt, tt), 1)
    strict_upper = jnp.where(r < c, 1.0, 0.0).astype(BF16)
    lane = lax.broadcasted_iota(jnp.int32, (ne, LANES), 1)
    cnt = jnp.zeros((ne, LANES), F32)

    def count(mask):
        return jnp.sum(jnp.where(mask, 1.0, 0.0), axis=1, keepdims=True)

    base = 0.0
    for lo, hi, cap in ((0, n_lat, cap_lat), (n_lat, lt, cap_ctx)):
        bits = pltpu.bitcast(affT_ref[:, lo:hi], jnp.int32)

        def bisect(step, thr, bits=bits, cap=cap):
            cand = thr | jnp.left_shift(jnp.int32(1), 30 - step)
            return jnp.where(count(bits >= cand) >= cap, cand, thr)

        thr = lax.fori_loop(0, 31, bisect, jnp.zeros((ne, 1), jnp.int32))
        need = cap - count(bits > thr)
        ties_before = jnp.zeros((ne, 1), F32)
        rows_before = jnp.zeros((ne, 1), F32) + base
        for ti in range((hi - lo) // tt):
            tile = lo // tt + ti
            b_t = bits[:, ti * tt:(ti + 1) * tt]
            eq = b_t == thr
            eq_f = jnp.where(eq, 1.0, 0.0)
            rank = jnp.dot(eq_f.astype(BF16), strict_upper, preferred_element_type=F32) + ties_before
            sel = (b_t > thr) | (eq & (rank < need))
            sel_f = jnp.where(sel, 1.0, 0.0)
            pos = jnp.dot(sel_f.astype(BF16), strict_upper, preferred_element_type=F32) + rows_before
            pos_ref[:, tile * tt:(tile + 1) * tt] = jnp.where(sel, pos, -1.0).astype(jnp.int32)
            cnt = jnp.where(lane == tile, rows_before, cnt)
            ties_before = ties_before + jnp.sum(eq_f, axis=1, keepdims=True)
            rows_before = rows_before + jnp.sum(sel_f, axis=1, keepdims=True)
        base = base + cap
    cnt = jnp.where(lane >= lt // tt, base, cnt)
    cnt_ref[...] = cnt.astype(jnp.int32)


def _route(affT, n_lat, cap_lat, cap_ctx):
    b, ne, lt = affT.shape
    return pl.pallas_call(
        functools.partial(_route_kernel, n_lat=n_lat, cap_lat=cap_lat, cap_ctx=cap_ctx),
        out_shape=(jax.ShapeDtypeStruct((b, ne, lt), jnp.int32),
                   jax.ShapeDtypeStruct((b, ne, LANES), jnp.int32)),
        grid=(b,),
        in_specs=[pl.BlockSpec((None, ne, lt), lambda i: (i, 0, 0))],
        out_specs=(pl.BlockSpec((None, ne, lt), lambda i: (i, 0, 0)),
                   pl.BlockSpec((None, ne, LANES), lambda i: (i, 0, 0))),
        compiler_params=_cparams(("arbitrary",), VMEM_LIMIT),
        name="route",
    )(affT)


def _gather_kernel(cnt_ref, pos_ref, h_ref, o_ref, acc_ref, *, cap):
    tt = TOKEN_TILE
    i, e, s = pl.program_id(0), pl.program_id(1), pl.program_id(2)
    sub = h_ref.shape[0] // tt
    base = (i * N_EXPERTS + e) * LANES + s * sub

    @pl.when(s == 0)
    def _():
        acc_ref[...] = jnp.zeros(acc_ref.shape, F32)

    for u in range(sub):
        c0 = cnt_ref[base + u]
        c1 = cnt_ref[base + u + 1]
        r0 = (c0 // 8) * 8
        n_win = jnp.where(c1 > c0, (c1 - r0 + ROW_WINDOW - 1) // ROW_WINDOW, 0)
        prow = pos_ref[:, u * tt:(u + 1) * tt]
        hb = h_ref[u * tt:(u + 1) * tt, :]

        def window(w, carry, r0=r0, prow=prow, hb=hb):
            rs = pl.multiple_of(r0 + w * ROW_WINDOW, 8)
            rows = rs + lax.broadcasted_iota(jnp.int32, (ROW_WINDOW, tt), 0)
            onehot = jnp.where(prow == rows, 1.0, 0.0).astype(BF16)
            acc_ref[pl.ds(rs, ROW_WINDOW), :] += jnp.dot(onehot, hb, preferred_element_type=F32)
            return carry

        lax.fori_loop(0, n_win, window, 0)

    @pl.when(s == pl.num_programs(2) - 1)
    def _():
        o_ref[...] = acc_ref[:cap, :].astype(BF16)


def _gather_tokens_per_step(nt):
    for k in (5, 4, 2, 1):
        if nt % k == 0:
            return k * TOKEN_TILE


def _gather(cnt_flat, pos4, h2, cap):
    b, lt, d = h2.shape
    gt = _gather_tokens_per_step(lt // TOKEN_TILE)
    grid_spec = pltpu.PrefetchScalarGridSpec(
        num_scalar_prefetch=1,
        grid=(b, N_EXPERTS, lt // gt),
        in_specs=[pl.BlockSpec((None, None, 1, gt), lambda i, e, s, cnt: (i, e, 0, s)),
                  pl.BlockSpec((None, gt, d), lambda i, e, s, cnt: (i, s, 0))],
        out_specs=pl.BlockSpec((None, None, cap, d), lambda i, e, s, cnt: (i, e, 0, 0)),
        scratch_shapes=[pltpu.VMEM((cap + ROW_WINDOW + 8, d), F32)],
    )
    return pl.pallas_call(
        functools.partial(_gather_kernel, cap=cap),
        out_shape=jax.ShapeDtypeStruct((b, N_EXPERTS, cap, d), BF16),
        grid_spec=grid_spec,
        compiler_params=_cparams(("parallel", "parallel", "arbitrary"), VMEM_LIMIT),
        name="moe_gather",
    )(cnt_flat, pos4, h2)


def _ffn_kernel(x_ref, wg_ref, wu_ref, wd_ref, y_ref, *, f_chunk):
    x = x_ref[...]
    acc = jnp.zeros((x.shape[0], wd_ref.shape[1]), F32)
    for f in range(wg_ref.shape[1] // f_chunk):
        sl = slice(f * f_chunk, (f + 1) * f_chunk)
        g = jnp.dot(x, wg_ref[:, sl], preferred_element_type=F32)
        u = jnp.dot(x, wu_ref[:, sl], preferred_element_type=F32)
        a = (g * (1.0 / (1.0 + jnp.exp(-g))) * u).astype(BF16)
        acc = acc + jnp.dot(a, wd_ref[sl, :], preferred_element_type=F32)
    y_ref[...] = acc.astype(BF16)


def _ffn(x_sel, wg, wu, wd):
    b, ne, cap, d = x_sel.shape
    f = wg.shape[2]
    n_row_tiles = 2 if cap % 32 == 0 else 1
    rt = cap // n_row_tiles
    return pl.pallas_call(
        functools.partial(_ffn_kernel, f_chunk=512),
        out_shape=jax.ShapeDtypeStruct((b, ne, cap, d), BF16),
        grid=(ne, b * n_row_tiles),
        in_specs=[
            pl.BlockSpec((None, None, rt, d), lambda e, r: (r // n_row_tiles, e, r % n_row_tiles, 0)),
            pl.BlockSpec((None, d, f), lambda e, r: (e, 0, 0)),
            pl.BlockSpec((None, d, f), lambda e, r: (e, 0, 0)),
            pl.BlockSpec((None, f, d), lambda e, r: (e, 0, 0)),
        ],
        out_specs=pl.BlockSpec((None, None, rt, d), lambda e, r: (r // n_row_tiles, e, r % n_row_tiles, 0)),
        compiler_params=_cparams(("parallel", "arbitrary"), VMEM_LIMIT),
        name="moe_ffn",
    )(x_sel, wg, wu, wd)


def _combine_kernel(cnt_ref, x_ref, pos_ref, aff_ref, mod_ref, *rest, cap, final):
    if final:
        gfin_ref, y_hbm, o_ref, buf_ref, xbuf_ref, acc_ref, sem_ref, xsem_ref = rest
    else:
        y_hbm, o_ref, buf_ref, xbuf_ref, acc_ref, sem_ref, xsem_ref = rest
    tt = TOKEN_TILE
    i, t = pl.program_id(0), pl.program_id(1)
    ne = pos_ref.shape[0]
    last_start = cap - ROW_WINDOW

    def first_window(e):
        c0 = cnt_ref[(i * ne + e) * LANES + t]
        return pl.multiple_of(jnp.minimum((c0 // 16) * 16, last_start), 16)

    def window_copy(e, rs):
        return pltpu.make_async_copy(y_hbm.at[i, e, pl.ds(rs, ROW_WINDOW), :], buf_ref.at[e], sem_ref.at[e])

    for e in range(ne):
        window_copy(e, first_window(e)).start()

    pos_f = pos_ref[...].astype(F32)
    pad = jnp.full((LANES - ne, tt), -1.0, F32)
    posT = jnp.concatenate([pos_f, pad], axis=0).T
    lane = lax.broadcasted_iota(jnp.int32, (1, ROW_WINDOW), 1).astype(F32)
    acc_ref[...] = jnp.zeros(acc_ref.shape, F32)

    for e in range(ne):
        rs = first_window(e)
        c1 = cnt_ref[(i * ne + e) * LANES + t + 1]
        pcol = posT[:, e:e + 1]
        gate = aff_ref[:, e:e + 1]
        window_copy(e, rs).wait()
        onehot = jnp.where(pcol == rs.astype(F32) + lane, 1.0, 0.0).astype(BF16)
        acc_ref[...] += gate * jnp.dot(onehot, buf_ref[e], preferred_element_type=F32)
        n_extra = jnp.maximum(c1 - rs - 1, 0) // ROW_WINDOW

        def extra(w, carry, e=e, rs=rs, pcol=pcol, gate=gate):
            want = rs + (w + 1) * ROW_WINDOW
            start = pl.multiple_of(jnp.minimum(want, last_start), 16)
            cp = pltpu.make_async_copy(y_hbm.at[i, e, pl.ds(start, ROW_WINDOW), :], xbuf_ref, xsem_ref)
            cp.start()
            cp.wait()
            row = start.astype(F32) + lane
            hit = (pcol == row) & (row >= want.astype(F32))
            acc_ref[...] += gate * jnp.dot(jnp.where(hit, 1.0, 0.0).astype(BF16), xbuf_ref[...],
                                           preferred_element_type=F32)
            return carry

        lax.fori_loop(0, n_extra, extra, 0)

    x2 = x_ref[...] + mod_ref[5:6, :] * acc_ref[...]
    if final:
        ms = jnp.mean(x2 * x2, axis=-1, keepdims=True)
        x2 = x2 * lax.rsqrt(ms + NORM_EPS) * gfin_ref[...]
    o_ref[...] = x2


def _combine(cnt_flat, x1, pos, aff, modsel, y, n_lat, final_g=None):
    b, lt, d = x1.shape
    tt = TOKEN_TILE
    nt = lt // tt
    cap = y.shape[2]
    final = final_g is not None
    n_steps = n_lat // tt if final else nt
    n_out = n_lat if final else lt
    in_specs = [
        pl.BlockSpec((None, tt, d), lambda i, t, cnt: (i, t, 0)),
        pl.BlockSpec((None, N_EXPERTS, tt), lambda i, t, cnt: (i, 0, t)),
        pl.BlockSpec((None, tt, LANES), lambda i, t, cnt: (i, t, 0)),
        pl.BlockSpec((None, None, N_MOD, d), lambda i, t, cnt: (i, jnp.where(t == nt - 1, 0, 1), 0, 0)),
    ]
    args = [x1, pos, aff, modsel]
    if final:
        in_specs.append(pl.BlockSpec((1, d), lambda i, t, cnt: (0, 0)))
        args.append(final_g)
    in_specs.append(pl.BlockSpec(memory_space=pl.ANY))
    args.append(y)
    grid_spec = pltpu.PrefetchScalarGridSpec(
        num_scalar_prefetch=1,
        grid=(b, n_steps),
        in_specs=in_specs,
        out_specs=pl.BlockSpec((None, tt, d), lambda i, t, cnt: (i, t, 0)),
        scratch_shapes=[
            pltpu.VMEM((N_EXPERTS, ROW_WINDOW, d), BF16),
            pltpu.VMEM((ROW_WINDOW, d), BF16),
            pltpu.VMEM((tt, d), F32),
            pltpu.SemaphoreType.DMA((N_EXPERTS,)),
            pltpu.SemaphoreType.DMA(()),
        ],
    )
    return pl.pallas_call(
        functools.partial(_combine_kernel, cap=cap, final=final),
        out_shape=jax.ShapeDtypeStruct((b, n_out, d), F32),
        grid_spec=grid_spec,
        compiler_params=_cparams(("arbitrary", "arbitrary"), VMEM_LIMIT),
        name="moe_combine",
    )(cnt_flat, *args)


def _rope_tables(n_lat, n_ctx):
    rows = n_lat // GRID_W
    row = jnp.repeat(jnp.arange(rows, dtype=F32), GRID_W)
    col = jnp.tile(jnp.arange(GRID_W, dtype=F32), rows)
    inv_freq = 1.0 / (ROPE_THETA ** (jnp.arange(0, ROPE_AXIS_DIM, 2, dtype=F32) / ROPE_AXIS_DIM))
    ang_r = (row[:, None] * inv_freq).T
    ang_c = (col[:, None] * inv_freq).T
    ct = jnp.concatenate([jnp.cos(ang_r), jnp.cos(ang_r), jnp.cos(ang_c), jnp.cos(ang_c)], axis=0)
    st = jnp.concatenate([-jnp.sin(ang_r), jnp.sin(ang_r), -jnp.sin(ang_c), jnp.sin(ang_c)], axis=0)
    ct = jnp.concatenate([ct, jnp.ones((HEAD_DIM, n_ctx), F32)], axis=1)
    st = jnp.concatenate([st, jnp.zeros((HEAD_DIM, n_ctx), F32)], axis=1)
    return ct, st


def kernel(x, c, ctx, c_ctx, w_mod, b_mod, norm_mix_g, norm_ffn_g, w_qkv, w_o, q_norm_g, k_norm_g,
           attn_sink, w_router, w_gate, w_up, w_down, final_norm_g):
    b, n_lat, d = x.shape
    n_ctx = ctx.shape[1]
    depth = w_mod.shape[0]
    tt = TOKEN_TILE
    assert n_ctx == tt and n_lat % tt == 0 and d == N_Q_HEADS * HEAD_DIM
    cap_lat = max(1, EC_CAPACITY * n_lat // N_EXPERTS)
    cap_ctx = max(1, EC_CAPACITY * n_ctx // N_EXPERTS)
    cap = cap_lat + cap_ctx
    assert cap % 16 == 0 and cap >= ROW_WINDOW and (cap - ROW_WINDOW) % 16 == 0

    cond = jnp.concatenate([c, c_ctx[None, :], jnp.zeros((8 - b - 1, d), F32)], axis=0)
    mod = _modulation(cond, w_mod, b_mod)
    mod = mod.reshape(depth, 8, N_MOD, d)
    modsel = jnp.stack([jnp.broadcast_to(mod[:, b][:, None], (depth, b, N_MOD, d)), mod[:, :b]], axis=2)

    ct, st = _rope_tables(n_lat, n_ctx)
    xs = jnp.concatenate([x, ctx], axis=1)
    wg, wu, wd = w_gate.astype(BF16), w_up.astype(BF16), w_down.astype(BF16)
    n_qk = (N_Q_HEADS + N_KV_HEADS) * HEAD_DIM

    for i in range(depth):
        last = i == depth - 1
        is_global = i % 2 == 0
        j = i // 2
        wT = w_qkv[i].T.astype(BF16)
        if is_global:
            gqk = jnp.concatenate([jnp.tile(q_norm_g[j], N_Q_HEADS), jnp.tile(k_norm_g[j], N_KV_HEADS)])
        else:
            gqk = jnp.ones((n_qk,), F32)
        gqk = jnp.broadcast_to(gqk[:, None], (n_qk, tt))
        qT, k, vT = _pre_attn(xs, modsel[i], norm_mix_g[i][None, :], wT, ct, st, gqk, is_global)
        if is_global:
            o_lat = _attn_global(qT, k, vT, n_lat)
            o_ctx = _attn_local(qT, k, vT, None, n_lat, band=False)
        else:
            sink = jnp.repeat(attn_sink[j].reshape(N_KV_HEADS, GQA_GROUP), tt, axis=1)[:, None, :]
            o_lat = _attn_local(qT, k, vT, sink, n_lat, band=True)
            o_ctx = _attn_local(qT, k, vT, sink, n_lat, band=False)
        wr = jnp.pad(w_router[i], ((0, 0), (0, LANES - N_EXPERTS)))
        x1, h2, aff, affT = _post_attn(o_lat.reshape(b, d, n_lat), o_ctx.reshape(b, d, n_ctx), xs, modsel[i],
                                       norm_ffn_g[i][None, :], w_o[i].T.astype(BF16), wr)
        pos, cnt = _route(affT, n_lat, cap_lat, cap_ctx)
        cnt_flat = cnt.reshape(-1)
        x_sel = _gather(cnt_flat, pos.reshape(b, N_EXPERTS, 1, n_lat + n_ctx), h2, cap)
        y = _ffn(x_sel, wg[i], wu[i], wd[i])
        xs = _combine(cnt_flat, x1, pos, aff, modsel[i], y, n_lat,
                      final_g=final_norm_g[None, :] if last else None)
    return xs
```

```python
import functools

import jax
import jax.numpy as jnp
from jax import lax
from jax.experimental import pallas as pl
from jax.experimental.pallas import tpu as pltpu

HEAD_DIM = 64
N_KV_HEADS = 4
GQA_GROUP = 4
N_Q_HEADS = N_KV_HEADS * GQA_GROUP
ROPE_AXIS_DIM = HEAD_DIM // 2
ROPE_HALF = ROPE_AXIS_DIM // 2
ROPE_THETA = 10000.0
GRID_W = 64
WINDOW = 128
N_EXPERTS = 16
EC_CAPACITY = 2
N_MOD = 6
NORM_EPS = 1e-6
NEG_BIG = -1e30
LOG2_E = 1.4426950408889634
Q_SCALE = HEAD_DIM ** -0.5 * LOG2_E

TOKEN_TILE = 256
LANES = 128
V_ROWS = 80
ROW_WINDOW = 128
GATHER_EXPERT_GROUP = 4
SOFTMAX_LAG = 1
PV_LAG = 3
VMEM_LIMIT = 56 * 1024 * 1024

F32 = jnp.float32
BF16 = jnp.bfloat16


def _cparams(sem, vmem=None):
    return pltpu.CompilerParams(dimension_semantics=sem, vmem_limit_bytes=vmem)


def _mod_kernel(c_ref, w_ref, b_ref, o_ref):
    c = c_ref[...]
    s = c * (1.0 / (1.0 + jnp.exp(-c)))
    o_ref[...] = jnp.dot(s, w_ref[...], preferred_element_type=F32,
                         precision=lax.Precision.HIGHEST) + b_ref[...]


def _modulation(cond, w_mod, b_mod):
    depth, d, n = w_mod.shape
    nb = n // 4
    return pl.pallas_call(
        _mod_kernel,
        out_shape=jax.ShapeDtypeStruct((depth, 8, n), F32),
        grid=(depth, n // nb),
        in_specs=[pl.BlockSpec((8, d), lambda l, j: (0, 0)),
                  pl.BlockSpec((None, d, nb), lambda l, j: (l, 0, j)),
                  pl.BlockSpec((None, 1, nb), lambda l, j: (l, 0, j))],
        out_specs=pl.BlockSpec((None, 8, nb), lambda l, j: (l, 0, j)),
        compiler_params=_cparams(("arbitrary", "arbitrary"), VMEM_LIMIT),
        name="modulation",
    )(cond, w_mod, b_mod.reshape(depth, 1, n))


def _pre_attn_kernel(x_ref, mod_ref, g_ref, wT_ref, ct_ref, st_ref, gqk_ref,
                     q_ref, k_ref, v_ref, *, qk_norm):
    tt = x_ref.shape[0]
    x = x_ref[...]
    ms = jnp.mean(x * x, axis=-1, keepdims=True)
    h = x * lax.rsqrt(ms + NORM_EPS) * g_ref[...]
    h = h * (1.0 + mod_ref[1:2, :]) + mod_ref[0:1, :]
    qkv = lax.dot_general(wT_ref[...], h.astype(BF16), (((1,), (1,)), ((), ())),
                          preferred_element_type=F32)
    n_qk = (N_Q_HEADS + N_KV_HEADS) * HEAD_DIM
    n_heads = N_Q_HEADS + N_KV_HEADS
    qk = qkv[:n_qk].reshape(n_heads, HEAD_DIM, tt)
    if qk_norm:
        ss = jnp.mean(qk * qk, axis=1, keepdims=True)
        qk = qk * lax.rsqrt(ss + NORM_EPS) * gqk_ref[...].reshape(n_heads, HEAD_DIM, tt)
    q5 = qk.reshape(n_heads, 2, 2, ROPE_HALF, tt)
    swapped = jnp.concatenate([q5[:, :, 1:2], q5[:, :, 0:1]], axis=2).reshape(n_heads, HEAD_DIM, tt)
    qk = qk * ct_ref[...][None] + swapped * st_ref[...][None]

    zeros64 = jnp.zeros((HEAD_DIM, tt), BF16)
    for hh in range(N_KV_HEADS):
        for g in range(GQA_GROUP):
            piece = (qk[hh * GQA_GROUP + g] * Q_SCALE).astype(BF16)
            pair = [piece, zeros64] if hh % 2 == 0 else [zeros64, piece]
            q_ref[hh, :, g * tt:(g + 1) * tt] = jnp.concatenate(pair, axis=0)
    kT = qk[N_Q_HEADS:].reshape(N_KV_HEADS * HEAD_DIM, tt)
    for p in range(N_KV_HEADS // 2):
        k_ref[p] = kT[p * LANES:(p + 1) * LANES].T.astype(BF16)
    v = qkv[n_qk:]
    row = lax.broadcasted_iota(jnp.int32, (V_ROWS - HEAD_DIM, tt), 0)
    tail = jnp.where(row == 0, 1.0, 0.0).astype(BF16)
    for hh in range(N_KV_HEADS):
        v_ref[hh] = jnp.concatenate([v[hh * HEAD_DIM:(hh + 1) * HEAD_DIM].astype(BF16), tail], axis=0)


def _pre_attn(xs, modsel, g, wT, ct, st, gqk, qk_norm):
    b, lt, d = xs.shape
    tt = TOKEN_TILE
    nt = lt // tt
    n_qkv = wT.shape[0]
    n_qk = gqk.shape[0]
    out_shape = (
        jax.ShapeDtypeStruct((b, N_KV_HEADS, nt, 2 * HEAD_DIM, GQA_GROUP * tt), BF16),
        jax.ShapeDtypeStruct((b, N_KV_HEADS // 2, lt, 2 * HEAD_DIM), BF16),
        jax.ShapeDtypeStruct((b, N_KV_HEADS, V_ROWS, lt), BF16),
    )
    return pl.pallas_call(
        functools.partial(_pre_attn_kernel, qk_norm=qk_norm),
        out_shape=out_shape,
        grid=(b, nt),
        in_specs=[
            pl.BlockSpec((None, tt, d), lambda i, t: (i, t, 0)),
            pl.BlockSpec((None, None, N_MOD, d), lambda i, t: (i, jnp.where(t == nt - 1, 0, 1), 0, 0)),
            pl.BlockSpec((1, d), lambda i, t: (0, 0)),
            pl.BlockSpec((n_qkv, d), lambda i, t: (0, 0)),
            pl.BlockSpec((HEAD_DIM, tt), lambda i, t: (0, t)),
            pl.BlockSpec((HEAD_DIM, tt), lambda i, t: (0, t)),
            pl.BlockSpec((n_qk, tt), lambda i, t: (0, 0)),
        ],
        out_specs=(
            pl.BlockSpec((None, N_KV_HEADS, None, 2 * HEAD_DIM, GQA_GROUP * tt), lambda i, t: (i, 0, t, 0, 0)),
            pl.BlockSpec((None, N_KV_HEADS // 2, tt, 2 * HEAD_DIM), lambda i, t: (i, 0, t, 0)),
            pl.BlockSpec((None, N_KV_HEADS, V_ROWS, tt), lambda i, t: (i, 0, 0, t)),
        ),
        compiler_params=_cparams(("parallel", "arbitrary"), VMEM_LIMIT),
        name="pre_attn",
    )(xs, modsel, g, wT, ct, st, gqk)


def _attn_global_kernel(q_ref, k_ref, v_ref, kc_ref, vc_ref, o_ref, m_ref, acc_ref, *, chunk):
    j = pl.program_id(3)
    n_tiles = q_ref.shape[0]
    tt = q_ref.shape[2] // GQA_GROUP
    tk = k_ref.shape[0]
    blocks = [(a, g) for a in range(n_tiles) for g in range(GQA_GROUP)]

    def cols(n):
        return slice(n * tt, (n + 1) * tt)

    def absorb(kr, vr, n_keys, step, m, acc):
        work = [(c, n) for c in range(n_keys // step) for n in range(len(blocks))]

        def scores(c, n):
            a, g = blocks[n]
            return jnp.dot(kr[c * step:(c + 1) * step, :], q_ref[a, :, cols(g)], preferred_element_type=F32)

        s_of, p_of, alpha_of = {}, {}, {}
        for t in range(len(work) + PV_LAG):
            if t < len(work):
                s_of[t] = scores(*work[t])
            if 0 <= t - SOFTMAX_LAG < len(work):
                u = t - SOFTMAX_LAG
                n = work[u][1]
                s = s_of.pop(u)
                m_new = jnp.maximum(m[n], jnp.max(s, axis=0, keepdims=True))
                p_of[u] = jnp.exp2(s - m_new).astype(BF16)
                alpha_of[u] = jnp.exp2(m[n] - m_new)
                m[n] = m_new
            if 0 <= t - PV_LAG < len(work):
                u = t - PV_LAG
                c, n = work[u]
                acc[n] = acc[n] * alpha_of.pop(u) + jnp.dot(vr[:, c * step:(c + 1) * step], p_of.pop(u),
                                                           preferred_element_type=F32)

    def save(m, acc):
        for n in range(len(blocks)):
            m_ref[:, cols(n)] = m[n]
            acc_ref[:, cols(n)] = acc[n]

    @pl.when(j == 0)
    def _():
        m0 = [jnp.full((1, tt), NEG_BIG, F32) for _ in blocks]
        acc0 = [jnp.zeros((V_ROWS, tt), F32) for _ in blocks]
        absorb(kc_ref, vc_ref, kc_ref.shape[0], kc_ref.shape[0], m0, acc0)
        save(m0, acc0)

    m = [m_ref[:, cols(n)] for n in range(len(blocks))]
    acc = [acc_ref[:, cols(n)] for n in range(len(blocks))]
    absorb(k_ref, v_ref, tk, chunk, m, acc)
    save(m, acc)

    @pl.when(j == pl.num_programs(3) - 1)
    def _():
        for n, (a, g) in enumerate(blocks):
            o = acc[n][:HEAD_DIM] / acc[n][HEAD_DIM:HEAD_DIM + 1]
            o_ref[g, :, a * tt:(a + 1) * tt] = o.astype(BF16)


def _attn_global(qT, k, vT, n_lat):
    b = qT.shape[0]
    tt = TOKEN_TILE
    n_lat_tiles = n_lat // tt
    q_tiles = 2 if n_lat_tiles % 2 == 0 else 1
    tk = 2048 if n_lat % 2048 == 0 else tt
    chunk = 512 if tk % 512 == 0 else tt
    tq = q_tiles * tt
    ctx_tile = n_lat // tt
    n_cols = q_tiles * GQA_GROUP * tt
    return pl.pallas_call(
        functools.partial(_attn_global_kernel, chunk=chunk),
        out_shape=jax.ShapeDtypeStruct((b, N_Q_HEADS, HEAD_DIM, n_lat), BF16),
        grid=(b, N_KV_HEADS, n_lat // tq, n_lat // tk),
        in_specs=[
            pl.BlockSpec((None, None, q_tiles, 2 * HEAD_DIM, GQA_GROUP * tt), lambda i, h, qi, kj: (i, h, qi, 0, 0)),
            pl.BlockSpec((None, None, tk, 2 * HEAD_DIM), lambda i, h, qi, kj: (i, h // 2, kj, 0)),
            pl.BlockSpec((None, None, V_ROWS, tk), lambda i, h, qi, kj: (i, h, 0, kj)),
            pl.BlockSpec((None, None, tt, 2 * HEAD_DIM), lambda i, h, qi, kj: (i, h // 2, ctx_tile, 0)),
            pl.BlockSpec((None, None, V_ROWS, tt), lambda i, h, qi, kj: (i, h, 0, ctx_tile)),
        ],
        out_specs=pl.BlockSpec((None, GQA_GROUP, HEAD_DIM, tq), lambda i, h, qi, kj: (i, h, 0, qi)),
        scratch_shapes=[pltpu.VMEM((1, n_cols), F32), pltpu.VMEM((V_ROWS, n_cols), F32)],
        compiler_params=_cparams(("parallel", "parallel", "parallel", "arbitrary"), VMEM_LIMIT),
        name="attn_global",
    )(qT, k, vT, k, vT)


def _attn_local_kernel(*refs, band, has_sink):
    refs = list(refs)
    q_ref = refs.pop(0)
    kc_ref, vc_ref = refs.pop(0), refs.pop(0)
    if band:
        kp_ref, k_ref, kn_ref, vp_ref, v_ref, vn_ref = [refs.pop(0) for _ in range(6)]
    if has_sink:
        sink_ref = refs.pop(0)
    o_ref = refs.pop(0)
    tt = q_ref.shape[1] // GQA_GROUP
    n_ctx = kc_ref.shape[0]
    q = q_ref[...]
    if band:
        k = jnp.concatenate([kc_ref[...], kp_ref[...], k_ref[...], kn_ref[...]], axis=0)
        vT = jnp.concatenate([vc_ref[...], vp_ref[...], v_ref[...], vn_ref[...]], axis=1)
    else:
        k, vT = kc_ref[...], vc_ref[...]
    s = jnp.dot(k, q, preferred_element_type=F32)
    if band:
        i = pl.program_id(2)
        kk = lax.broadcasted_iota(jnp.int32, s.shape, 0) - n_ctx
        t = lax.broadcasted_iota(jnp.int32, s.shape, 1) % tt
        lo = jnp.where(i == 0, WINDOW, 0)
        hi = jnp.where(i == pl.num_programs(2) - 1, WINDOW + tt, 2 * WINDOW + tt)
        ok = (kk < 0) | ((kk >= t) & (kk <= t + 2 * WINDOW) & (kk >= lo) & (kk < hi))
        s = jnp.where(ok, s, NEG_BIG)
    m = jnp.max(s, axis=0, keepdims=True)
    if has_sink:
        sink = sink_ref[...]
        m = jnp.maximum(m, sink)
    p = jnp.exp2(s - m).astype(BF16)
    acc = jnp.dot(vT, p, preferred_element_type=F32)
    denom = acc[HEAD_DIM:HEAD_DIM + 1]
    if has_sink:
        denom = denom + jnp.exp2(sink - m)
    o = acc[:HEAD_DIM] / denom
    for g in range(GQA_GROUP):
        o_ref[g] = o[:, g * tt:(g + 1) * tt].astype(BF16)


def _attn_local(qT, k, vT, sink, n_lat, band):
    b = qT.shape[0]
    tt = TOKEN_TILE
    ctx_tile = n_lat // tt
    n_steps = n_lat // tt if band else 1
    half = tt // WINDOW
    last_blk = n_lat // WINDOW - 1
    has_sink = sink is not None
    q_idx = (lambda i, h, t: (i, h, t, 0, 0)) if band else (lambda i, h, t: (i, h, ctx_tile, 0, 0))
    in_specs = [
        pl.BlockSpec((None, None, None, 2 * HEAD_DIM, GQA_GROUP * tt), q_idx),
        pl.BlockSpec((None, None, tt, 2 * HEAD_DIM), lambda i, h, t: (i, h // 2, ctx_tile, 0)),
        pl.BlockSpec((None, None, V_ROWS, tt), lambda i, h, t: (i, h, 0, ctx_tile)),
    ]
    args = [qT, k, vT]
    if band:
        prev = lambda t: jnp.maximum(t * half - 1, 0)
        nxt = lambda t: jnp.minimum((t + 1) * half, last_blk)
        in_specs += [
            pl.BlockSpec((None, None, WINDOW, 2 * HEAD_DIM), lambda i, h, t: (i, h // 2, prev(t), 0)),
            pl.BlockSpec((None, None, tt, 2 * HEAD_DIM), lambda i, h, t: (i, h // 2, t, 0)),
            pl.BlockSpec((None, None, WINDOW, 2 * HEAD_DIM), lambda i, h, t: (i, h // 2, nxt(t), 0)),
            pl.BlockSpec((None, None, V_ROWS, WINDOW), lambda i, h, t: (i, h, 0, prev(t))),
            pl.BlockSpec((None, None, V_ROWS, tt), lambda i, h, t: (i, h, 0, t)),
            pl.BlockSpec((None, None, V_ROWS, WINDOW), lambda i, h, t: (i, h, 0, nxt(t))),
        ]
        args += [k, k, k, vT, vT, vT]
    if has_sink:
        in_specs.append(pl.BlockSpec((None, 1, GQA_GROUP * tt), lambda i, h, t: (h, 0, 0)))
        args.append(sink)
    n_out = n_lat if band else tt
    return pl.pallas_call(
        functools.partial(_attn_local_kernel, band=band, has_sink=has_sink),
        out_shape=jax.ShapeDtypeStruct((b, N_Q_HEADS, HEAD_DIM, n_out), BF16),
        grid=(b, N_KV_HEADS, n_steps),
        in_specs=in_specs,
        out_specs=pl.BlockSpec((None, GQA_GROUP, HEAD_DIM, tt), lambda i, h, t: (i, h, 0, t)),
        compiler_params=_cparams(("parallel", "parallel", "arbitrary"), VMEM_LIMIT),
        name="attn_window" if band else "attn_ctx",
    )(*args)


def _post_attn_kernel(ol_ref, oc_ref, x_ref, mod_ref, g_ref, woT_ref, wr_ref,
                      x1_ref, h2_ref, aff_ref, affT_ref):
    t = pl.program_id(1)
    is_ctx = t == pl.num_programs(1) - 1
    oT = jnp.where(is_ctx, oc_ref[...], ol_ref[...])
    yT = jnp.dot(woT_ref[...], oT, preferred_element_type=F32)
    x1 = x_ref[...] + mod_ref[2:3, :] * yT.T
    x1_ref[...] = x1
    ms = jnp.mean(x1 * x1, axis=-1, keepdims=True)
    h2 = x1 * lax.rsqrt(ms + NORM_EPS) * g_ref[...]
    h2 = h2 * (1.0 + mod_ref[4:5, :]) + mod_ref[3:4, :]
    h2_ref[...] = h2.astype(BF16)
    logits = jnp.dot(h2, wr_ref[...], preferred_element_type=F32, precision=lax.Precision.HIGHEST)
    lane = lax.broadcasted_iota(jnp.int32, logits.shape, 1)
    logits = jnp.where(lane < N_EXPERTS, logits, NEG_BIG)
    e = jnp.exp(logits - jnp.max(logits, axis=-1, keepdims=True))
    aff = e / jnp.sum(e, axis=-1, keepdims=True)
    aff_ref[...] = aff
    affT_ref[...] = aff.T[:N_EXPERTS]


def _post_attn(o_lat, o_ctx, xs, modsel, g, woT, wr):
    b, lt, d = xs.shape
    tt = TOKEN_TILE
    nt = lt // tt
    n_lat_tiles = nt - 1
    out_shape = (
        jax.ShapeDtypeStruct((b, lt, d), F32),
        jax.ShapeDtypeStruct((b, lt, d), BF16),
        jax.ShapeDtypeStruct((b, lt, LANES), F32),
        jax.ShapeDtypeStruct((b, N_EXPERTS, lt), F32),
    )
    return pl.pallas_call(
        _post_attn_kernel,
        out_shape=out_shape,
        grid=(b, nt),
        in_specs=[
            pl.BlockSpec((None, d, tt), lambda i, t: (i, 0, jnp.minimum(t, n_lat_tiles - 1))),
            pl.BlockSpec((None, d, tt), lambda i, t: (i, 0, 0)),
            pl.BlockSpec((None, tt, d), lambda i, t: (i, t, 0)),
            pl.BlockSpec((None, None, N_MOD, d), lambda i, t: (i, jnp.where(t == nt - 1, 0, 1), 0, 0)),
            pl.BlockSpec((1, d), lambda i, t: (0, 0)),
            pl.BlockSpec((d, d), lambda i, t: (0, 0)),
            pl.BlockSpec((d, LANES), lambda i, t: (0, 0)),
        ],
        out_specs=(
            pl.BlockSpec((None, tt, d), lambda i, t: (i, t, 0)),
            pl.BlockSpec((None, tt, d), lambda i, t: (i, t, 0)),
            pl.BlockSpec((None, tt, LANES), lambda i, t: (i, t, 0)),
            pl.BlockSpec((None, N_EXPERTS, tt), lambda i, t: (i, 0, t)),
        ),
        compiler_params=_cparams(("parallel", "arbitrary"), VMEM_LIMIT),
        name="post_attn",
    )(o_lat, o_ctx, xs, modsel, g, woT, wr)


def _route_kernel(affT_ref, pos_ref, cnt_ref, *, n_lat, cap_lat, cap_ctx):
    tt = TOKEN_TILE
    lt = affT_ref.shape[1]
    ne = affT_ref.shape[0]
    r = lax.broadcasted_iota(jnp.int32, (tt, tt), 0)
    c = lax.broadcasted_iota(jnp.int32, (tt, tt), 1)
    strict_upper = jnp.where(r < c, 1.0, 0.0).astype(BF16)
    lane = lax.broadcasted_iota(jnp.int32, (ne, LANES), 1)
    cnt = jnp.zeros((ne, LANES), F32)

    def count(mask):
        return jnp.sum(jnp.where(mask, 1.0, 0.0), axis=1, keepdims=True)

    base = 0.0
    for lo, hi, cap in ((0, n_lat, cap_lat), (n_lat, lt, cap_ctx)):
        bits = pltpu.bitcast(affT_ref[:, lo:hi], jnp.int32)

        def bisect(step, thr, bits=bits, cap=cap):
            cand = thr | jnp.left_shift(jnp.int32(1), 30 - step)
            return jnp.where(count(bits >= cand) >= cap, cand, thr)

        thr = lax.fori_loop(0, 31, bisect, jnp.zeros((ne, 1), jnp.int32))
        need = cap - count(bits > thr)
        ties_before = jnp.zeros((ne, 1), F32)
        rows_before = jnp.zeros((ne, 1), F32) + base
        for ti in range((hi - lo) // tt):
            tile = lo // tt + ti
            b_t = bits[:, ti * tt:(ti + 1) * tt]
            eq = b_t == thr
            eq_f = jnp.where(eq, 1.0, 0.0)
            rank = jnp.dot(eq_f.astype(BF16), strict_upper, preferred_element_type=F32) + ties_before
            sel = (b_t > thr) | (eq & (rank < need))
            sel_f = jnp.where(sel, 1.0, 0.0)
            pos = jnp.dot(sel_f.astype(BF16), strict_upper, preferred_element_type=F32) + rows_before
            pos_ref[:, tile * tt:(tile + 1) * tt] = jnp.where(sel, pos, -1.0).astype(jnp.int32)
            cnt = jnp.where(lane == tile, rows_before, cnt)
            ties_before = ties_before + jnp.sum(eq_f, axis=1, keepdims=True)
            rows_before = rows_before + jnp.sum(sel_f, axis=1, keepdims=True)
        base = base + cap
    cnt = jnp.where(lane >= lt // tt, base, cnt)
    cnt_ref[...] = cnt.astype(jnp.int32)


def _route(affT, n_lat, cap_lat, cap_ctx):
    b, ne, lt = affT.shape
    return pl.pallas_call(
        functools.partial(_route_kernel, n_lat=n_lat, cap_lat=cap_lat, cap_ctx=cap_ctx),
        out_shape=(jax.ShapeDtypeStruct((b, ne, lt), jnp.int32),
                   jax.ShapeDtypeStruct((b, ne, LANES), jnp.int32)),
        grid=(b,),
        in_specs=[pl.BlockSpec((None, ne, lt), lambda i: (i, 0, 0))],
        out_specs=(pl.BlockSpec((None, ne, lt), lambda i: (i, 0, 0)),
                   pl.BlockSpec((None, ne, LANES), lambda i: (i, 0, 0))),
        compiler_params=_cparams(("arbitrary",), VMEM_LIMIT),
        name="route",
    )(affT)


def _gather_kernel(cnt_ref, pos_ref, h_ref, o_ref, *, cap):
    tt = TOKEN_TILE
    i, eg, s = pl.program_id(0), pl.program_id(1), pl.program_id(2)
    ng = o_ref.shape[0]
    sub = h_ref.shape[0] // tt
    last_start = cap - ROW_WINDOW

    @pl.when(s == 0)
    def _():
        o_ref[...] = jnp.zeros(o_ref.shape, BF16)

    for u in range(sub):
        r0, n_win = [], 0
        for k in range(ng):
            base = (i * N_EXPERTS + eg * ng + k) * LANES + s * sub + u
            c0, c1 = cnt_ref[base], cnt_ref[base + 1]
            r0.append((c0 // 16) * 16)
            n_win = jnp.maximum(n_win, jnp.where(c1 > c0, (c1 - r0[k] + ROW_WINDOW - 1) // ROW_WINDOW, 0))
        hb = h_ref[u * tt:(u + 1) * tt, :]

        def window(w, carry, r0=r0, hb=hb, u=u):
            starts, pieces = [], []
            for k in range(ng):
                want = r0[k] + w * ROW_WINDOW
                rs = pl.multiple_of(jnp.minimum(want, last_start), 16)
                rows = rs + lax.broadcasted_iota(jnp.int32, (ROW_WINDOW, tt), 0)
                hit = (pos_ref[k, :, u * tt:(u + 1) * tt] == rows) & (rows >= want)
                pieces.append(jnp.where(hit, 1.0, 0.0).astype(BF16))
                starts.append(rs)
            got = jnp.dot(jnp.concatenate(pieces, axis=0), hb, preferred_element_type=F32)
            for k in range(ng):
                o_ref[k, pl.ds(starts[k], ROW_WINDOW), :] += got[k * ROW_WINDOW:(k + 1) * ROW_WINDOW].astype(BF16)
            return carry

        lax.fori_loop(0, n_win, window, 0)


def _gather_tokens_per_step(nt):
    for k in (5, 4, 2, 1):
        if nt % k == 0:
            return k * TOKEN_TILE


def _gather(cnt_flat, pos4, h2, cap):
    b, lt, d = h2.shape
    gt = _gather_tokens_per_step(lt // TOKEN_TILE)
    ng = GATHER_EXPERT_GROUP
    grid_spec = pltpu.PrefetchScalarGridSpec(
        num_scalar_prefetch=1,
        grid=(b, N_EXPERTS // ng, lt // gt),
        in_specs=[pl.BlockSpec((None, ng, 1, gt), lambda i, e, s, cnt: (i, e, 0, s)),
                  pl.BlockSpec((None, gt, d), lambda i, e, s, cnt: (i, s, 0))],
        out_specs=pl.BlockSpec((None, ng, cap, d), lambda i, e, s, cnt: (i, e, 0, 0)),
    )
    return pl.pallas_call(
        functools.partial(_gather_kernel, cap=cap),
        out_shape=jax.ShapeDtypeStruct((b, N_EXPERTS, cap, d), BF16),
        grid_spec=grid_spec,
        compiler_params=_cparams(("parallel", "parallel", "arbitrary"), VMEM_LIMIT),
        name="moe_gather",
    )(cnt_flat, pos4, h2)


def _ffn_kernel(x_ref, wg_ref, wu_ref, wd_ref, y_ref, *, f_chunk):
    x = x_ref[...]
    acc = jnp.zeros((x.shape[0], wd_ref.shape[1]), F32)
    for f in range(wg_ref.shape[1] // f_chunk):
        sl = slice(f * f_chunk, (f + 1) * f_chunk)
        g = jnp.dot(x, wg_ref[:, sl], preferred_element_type=F32)
        u = jnp.dot(x, wu_ref[:, sl], preferred_element_type=F32)
        a = (g * (1.0 / (1.0 + jnp.exp(-g))) * u).astype(BF16)
        acc = acc + jnp.dot(a, wd_ref[sl, :], preferred_element_type=F32)
    y_ref[...] = acc.astype(BF16)


def _ffn(x_sel, wg, wu, wd, layer):
    b, ne, cap, d = x_sel.shape
    f = wg.shape[3]
    n_row_tiles = 2 if cap % 32 == 0 else 1
    rt = cap // n_row_tiles
    return pl.pallas_call(
        functools.partial(_ffn_kernel, f_chunk=512),
        out_shape=jax.ShapeDtypeStruct((b, ne, cap, d), BF16),
        grid=(ne, b * n_row_tiles),
        in_specs=[
            pl.BlockSpec((None, None, rt, d), lambda e, r: (r // n_row_tiles, e, r % n_row_tiles, 0)),
            pl.BlockSpec((None, None, d, f), lambda e, r: (layer, e, 0, 0)),
            pl.BlockSpec((None, None, d, f), lambda e, r: (layer, e, 0, 0)),
            pl.BlockSpec((None, None, f, d), lambda e, r: (layer, e, 0, 0)),
        ],
        out_specs=pl.BlockSpec((None, None, rt, d), lambda e, r: (r // n_row_tiles, e, r % n_row_tiles, 0)),
        compiler_params=_cparams(("parallel", "arbitrary"), VMEM_LIMIT),
        name="moe_ffn",
    )(x_sel, wg, wu, wd)


def _combine_kernel(cnt_ref, x_ref, pos_ref, aff_ref, mod_ref, *rest, cap, final):
    if final:
        gfin_ref, y_hbm, o_ref, buf_ref, xbuf_ref, acc_ref, sem_ref, xsem_ref = rest
    else:
        y_hbm, o_ref, buf_ref, xbuf_ref, acc_ref, sem_ref, xsem_ref = rest
    tt = TOKEN_TILE
    i, t = pl.program_id(0), pl.program_id(1)
    n_t = pl.num_programs(1)
    ne = pos_ref.shape[0]
    last_start = cap - ROW_WINDOW
    step = i * n_t + t
    slot = step % 2

    def first_window(bi, ti, e):
        c0 = cnt_ref[(bi * ne + e) * LANES + ti]
        return pl.multiple_of(jnp.minimum((c0 // 16) * 16, last_start), 16)

    def window_copy(bi, ti, e, sl):
        rs = first_window(bi, ti, e)
        return pltpu.make_async_copy(y_hbm.at[bi, e, pl.ds(rs, ROW_WINDOW), :], buf_ref.at[sl, e], sem_ref.at[sl, e])

    @pl.when(step == 0)
    def _():
        for e in range(ne):
            window_copy(i, t, e, slot).start()

    @pl.when(step + 1 < pl.num_programs(0) * n_t)
    def _():
        wrap = t + 1 == n_t
        for e in range(ne):
            window_copy(jnp.where(wrap, i + 1, i), jnp.where(wrap, 0, t + 1), e, 1 - slot).start()

    pos_f = pos_ref[...].astype(F32)
    pad = jnp.full((LANES - ne, tt), -1.0, F32)
    posT = jnp.concatenate([pos_f, pad], axis=0).T
    lane = lax.broadcasted_iota(jnp.int32, (1, ROW_WINDOW), 1).astype(F32)

    hi, lo = [], []
    for e in range(ne):
        hit = posT[:, e:e + 1] == first_window(i, t, e).astype(F32) + lane
        gated = jnp.where(hit, aff_ref[:, e:e + 1], 0.0)
        g_hi = gated.astype(BF16)
        hi.append(g_hi)
        lo.append((gated - g_hi.astype(F32)).astype(BF16))
    for e in range(ne):
        window_copy(i, t, e, slot).wait()
    rows = buf_ref[slot].reshape(ne * ROW_WINDOW, buf_ref.shape[-1])
    acc_ref[...] = (jnp.dot(jnp.concatenate(hi, axis=1), rows, preferred_element_type=F32)
                    + jnp.dot(jnp.concatenate(lo, axis=1), rows, preferred_element_type=F32))

    for e in range(ne):
        rs = first_window(i, t, e)
        c1 = cnt_ref[(i * ne + e) * LANES + t + 1]
        pcol = posT[:, e:e + 1]
        gate = aff_ref[:, e:e + 1]
        n_extra = jnp.maximum(c1 - rs - 1, 0) // ROW_WINDOW

        def extra(w, carry, e=e, rs=rs, pcol=pcol, gate=gate):
            want = rs + (w + 1) * ROW_WINDOW
            start = pl.multiple_of(jnp.minimum(want, last_start), 16)
            cp = pltpu.make_async_copy(y_hbm.at[i, e, pl.ds(start, ROW_WINDOW), :], xbuf_ref, xsem_ref)
            cp.start()
            cp.wait()
            row = start.astype(F32) + lane
            hit = (pcol == row) & (row >= want.astype(F32))
            acc_ref[...] += gate * jnp.dot(jnp.where(hit, 1.0, 0.0).astype(BF16), xbuf_ref[...],
                                           preferred_element_type=F32)
            return carry

        lax.fori_loop(0, n_extra, extra, 0)

    x2 = x_ref[...] + mod_ref[5:6, :] * acc_ref[...]
    if final:
        ms = jnp.mean(x2 * x2, axis=-1, keepdims=True)
        x2 = x2 * lax.rsqrt(ms + NORM_EPS) * gfin_ref[...]
    o_ref[...] = x2


def _combine(cnt_flat, x1, pos, aff, modsel, y, n_lat, final_g=None):
    b, lt, d = x1.shape
    tt = TOKEN_TILE
    nt = lt // tt
    cap = y.shape[2]
    final = final_g is not None
    n_steps = n_lat // tt if final else nt
    n_out = n_lat if final else lt
    in_specs = [
        pl.BlockSpec((None, tt, d), lambda i, t, cnt: (i, t, 0)),
        pl.BlockSpec((None, N_EXPERTS, tt), lambda i, t, cnt: (i, 0, t)),
        pl.BlockSpec((None, tt, LANES), lambda i, t, cnt: (i, t, 0)),
        pl.BlockSpec((None, None, N_MOD, d), lambda i, t, cnt: (i, jnp.where(t == nt - 1, 0, 1), 0, 0)),
    ]
    args = [x1, pos, aff, modsel]
    if final:
        in_specs.append(pl.BlockSpec((1, d), lambda i, t, cnt: (0, 0)))
        args.append(final_g)
    in_specs.append(pl.BlockSpec(memory_space=pl.ANY))
    args.append(y)
    grid_spec = pltpu.PrefetchScalarGridSpec(
        num_scalar_prefetch=1,
        grid=(b, n_steps),
        in_specs=in_specs,
        out_specs=pl.BlockSpec((None, tt, d), lambda i, t, cnt: (i, t, 0)),
        scratch_shapes=[
            pltpu.VMEM((2, N_EXPERTS, ROW_WINDOW, d), BF16),
            pltpu.VMEM((ROW_WINDOW, d), BF16),
            pltpu.VMEM((tt, d), F32),
            pltpu.SemaphoreType.DMA((2, N_EXPERTS)),
            pltpu.SemaphoreType.DMA(()),
        ],
    )
    return pl.pallas_call(
        functools.partial(_combine_kernel, cap=cap, final=final),
        out_shape=jax.ShapeDtypeStruct((b, n_out, d), F32),
        grid_spec=grid_spec,
        compiler_params=_cparams(("arbitrary", "arbitrary"), VMEM_LIMIT),
        name="moe_combine",
    )(cnt_flat, *args)


def _rope_tables(n_lat, n_ctx):
    rows = n_lat // GRID_W
    row = jnp.repeat(jnp.arange(rows, dtype=F32), GRID_W)
    col = jnp.tile(jnp.arange(GRID_W, dtype=F32), rows)
    inv_freq = 1.0 / (ROPE_THETA ** (jnp.arange(0, ROPE_AXIS_DIM, 2, dtype=F32) / ROPE_AXIS_DIM))
    ang_r = (row[:, None] * inv_freq).T
    ang_c = (col[:, None] * inv_freq).T
    ct = jnp.concatenate([jnp.cos(ang_r), jnp.cos(ang_r), jnp.cos(ang_c), jnp.cos(ang_c)], axis=0)
    st = jnp.concatenate([-jnp.sin(ang_r), jnp.sin(ang_r), -jnp.sin(ang_c), jnp.sin(ang_c)], axis=0)
    ct = jnp.concatenate([ct, jnp.ones((HEAD_DIM, n_ctx), F32)], axis=1)
    st = jnp.concatenate([st, jnp.zeros((HEAD_DIM, n_ctx), F32)], axis=1)
    return ct, st


def kernel(x, c, ctx, c_ctx, w_mod, b_mod, norm_mix_g, norm_ffn_g, w_qkv, w_o, q_norm_g, k_norm_g,
           attn_sink, w_router, w_gate, w_up, w_down, final_norm_g):
    b, n_lat, d = x.shape
    n_ctx = ctx.shape[1]
    depth = w_mod.shape[0]
    tt = TOKEN_TILE
    assert n_ctx == tt and n_lat % tt == 0 and d == N_Q_HEADS * HEAD_DIM
    cap_lat = max(1, EC_CAPACITY * n_lat // N_EXPERTS)
    cap_ctx = max(1, EC_CAPACITY * n_ctx // N_EXPERTS)
    cap = cap_lat + cap_ctx
    assert cap % 16 == 0 and cap >= ROW_WINDOW and (cap - ROW_WINDOW) % 16 == 0

    cond = jnp.concatenate([c, c_ctx[None, :], jnp.zeros((8 - b - 1, d), F32)], axis=0)
    mod = _modulation(cond, w_mod, b_mod)
    mod = mod.reshape(depth, 8, N_MOD, d)
    modsel = jnp.stack([jnp.broadcast_to(mod[:, b][:, None], (depth, b, N_MOD, d)), mod[:, :b]], axis=2)

    ct, st = _rope_tables(n_lat, n_ctx)
    xs = jnp.concatenate([x, ctx], axis=1)
    wg, wu, wd = w_gate.astype(BF16), w_up.astype(BF16), w_down.astype(BF16)
    n_qk = (N_Q_HEADS + N_KV_HEADS) * HEAD_DIM

    for i in range(depth):
        last = i == depth - 1
        is_global = i % 2 == 0
        j = i // 2
        wT = w_qkv[i].T.astype(BF16)
        if is_global:
            gqk = jnp.concatenate([jnp.tile(q_norm_g[j], N_Q_HEADS), jnp.tile(k_norm_g[j], N_KV_HEADS)])
        else:
            gqk = jnp.ones((n_qk,), F32)
        gqk = jnp.broadcast_to(gqk[:, None], (n_qk, tt))
        qT, k, vT = _pre_attn(xs, modsel[i], norm_mix_g[i][None, :], wT, ct, st, gqk, is_global)
        if is_global:
            o_lat = _attn_global(qT, k, vT, n_lat)
            o_ctx = _attn_local(qT, k, vT, None, n_lat, band=False)
        else:
            sink = jnp.repeat(attn_sink[j].reshape(N_KV_HEADS, GQA_GROUP) * LOG2_E, tt, axis=1)[:, None, :]
            o_lat = _attn_local(qT, k, vT, sink, n_lat, band=True)
            o_ctx = _attn_local(qT, k, vT, sink, n_lat, band=False)
        wr = jnp.pad(w_router[i], ((0, 0), (0, LANES - N_EXPERTS)))
        x1, h2, aff, affT = _post_attn(o_lat.reshape(b, d, n_lat), o_ctx.reshape(b, d, n_ctx), xs, modsel[i],
                                       norm_ffn_g[i][None, :], w_o[i].T.astype(BF16), wr)
        pos, cnt = _route(affT, n_lat, cap_lat, cap_ctx)
        cnt_flat = cnt.reshape(-1)
        x_sel = _gather(cnt_flat, pos.reshape(b, N_EXPERTS, 1, n_lat + n_ctx), h2, cap)
        y = _ffn(x_sel, wg, wu, wd, i)
        xs = _combine(cnt_flat, x1, pos, aff, modsel[i], y, n_lat,
                      final_g=final_norm_g[None, :] if last else None)
    return xs
```

```python
import functools

import jax
import jax.numpy as jnp
from jax import lax
from jax.experimental import pallas as pl
from jax.experimental.pallas import tpu as pltpu

HEAD_DIM = 64
N_KV_HEADS = 4
GQA_GROUP = 4
N_Q_HEADS = N_KV_HEADS * GQA_GROUP
ROPE_AXIS_DIM = HEAD_DIM // 2
ROPE_HALF = ROPE_AXIS_DIM // 2
ROPE_THETA = 10000.0
GRID_W = 64
WINDOW = 128
N_EXPERTS = 16
EC_CAPACITY = 2
N_MOD = 6
NORM_EPS = 1e-6
NEG_BIG = -1e30
LOG2_E = 1.4426950408889634
Q_SCALE = HEAD_DIM ** -0.5 * LOG2_E

TOKEN_TILE = 256
LANES = 128
V_ROWS = 80
ROW_WINDOW = 128
FFN_SLICE = 512
GATHER_EXPERT_GROUP = 4
SOFTMAX_LAG = 1
PV_LAG = 3
VMEM_LIMIT = 56 * 1024 * 1024

F32 = jnp.float32
BF16 = jnp.bfloat16


def _cparams(sem, vmem=None):
    return pltpu.CompilerParams(dimension_semantics=sem, vmem_limit_bytes=vmem)


def _mod_kernel(c_ref, w_ref, b_ref, o_ref):
    c = c_ref[...]
    s = c * (1.0 / (1.0 + jnp.exp(-c)))
    o_ref[...] = jnp.dot(s, w_ref[...], preferred_element_type=F32,
                         precision=lax.Precision.HIGHEST) + b_ref[...]


def _modulation(cond, w_mod, b_mod):
    depth, d, n = w_mod.shape
    nb = n // 4
    return pl.pallas_call(
        _mod_kernel,
        out_shape=jax.ShapeDtypeStruct((depth, 8, n), F32),
        grid=(depth, n // nb),
        in_specs=[pl.BlockSpec((8, d), lambda l, j: (0, 0)),
                  pl.BlockSpec((None, d, nb), lambda l, j: (l, 0, j)),
                  pl.BlockSpec((None, 1, nb), lambda l, j: (l, 0, j))],
        out_specs=pl.BlockSpec((None, 8, nb), lambda l, j: (l, 0, j)),
        compiler_params=_cparams(("arbitrary", "arbitrary"), VMEM_LIMIT),
        name="modulation",
    )(cond, w_mod, b_mod.reshape(depth, 1, n))


def _pre_attn_kernel(x_ref, mod_ref, g_ref, wT_ref, ct_ref, st_ref, gqk_ref,
                     q_ref, k_ref, v_ref, *, qk_norm):
    tt = x_ref.shape[0]
    x = x_ref[...]
    ms = jnp.mean(x * x, axis=-1, keepdims=True)
    h = x * lax.rsqrt(ms + NORM_EPS) * g_ref[...]
    h = h * (1.0 + mod_ref[1:2, :]) + mod_ref[0:1, :]
    qkv = lax.dot_general(wT_ref[...], h.astype(BF16), (((1,), (1,)), ((), ())),
                          preferred_element_type=F32)
    n_qk = (N_Q_HEADS + N_KV_HEADS) * HEAD_DIM
    n_heads = N_Q_HEADS + N_KV_HEADS
    qk = qkv[:n_qk].reshape(n_heads, HEAD_DIM, tt)
    if qk_norm:
        ss = jnp.mean(qk * qk, axis=1, keepdims=True)
        qk = qk * lax.rsqrt(ss + NORM_EPS) * gqk_ref[...].reshape(n_heads, HEAD_DIM, tt)
    q5 = qk.reshape(n_heads, 2, 2, ROPE_HALF, tt)
    swapped = jnp.concatenate([q5[:, :, 1:2], q5[:, :, 0:1]], axis=2).reshape(n_heads, HEAD_DIM, tt)
    qk = qk * ct_ref[...][None] + swapped * st_ref[...][None]

    zeros64 = jnp.zeros((HEAD_DIM, tt), BF16)
    for hh in range(N_KV_HEADS):
        for g in range(GQA_GROUP):
            piece = (qk[hh * GQA_GROUP + g] * Q_SCALE).astype(BF16)
            pair = [piece, zeros64] if hh % 2 == 0 else [zeros64, piece]
            q_ref[hh, :, g * tt:(g + 1) * tt] = jnp.concatenate(pair, axis=0)
    kT = qk[N_Q_HEADS:].reshape(N_KV_HEADS * HEAD_DIM, tt)
    for p in range(N_KV_HEADS // 2):
        k_ref[p] = kT[p * LANES:(p + 1) * LANES].T.astype(BF16)
    v = qkv[n_qk:]
    row = lax.broadcasted_iota(jnp.int32, (V_ROWS - HEAD_DIM, tt), 0)
    tail = jnp.where(row == 0, 1.0, 0.0).astype(BF16)
    for hh in range(N_KV_HEADS):
        v_ref[hh] = jnp.concatenate([v[hh * HEAD_DIM:(hh + 1) * HEAD_DIM].astype(BF16), tail], axis=0)


def _pre_attn(xs, modsel, g, wT, ct, st, gqk, qk_norm):
    b, lt, d = xs.shape
    tt = TOKEN_TILE
    nt = lt // tt
    n_qkv = wT.shape[0]
    n_qk = gqk.shape[0]
    out_shape = (
        jax.ShapeDtypeStruct((b, N_KV_HEADS, nt, 2 * HEAD_DIM, GQA_GROUP * tt), BF16),
        jax.ShapeDtypeStruct((b, N_KV_HEADS // 2, lt, 2 * HEAD_DIM), BF16),
        jax.ShapeDtypeStruct((b, N_KV_HEADS, V_ROWS, lt), BF16),
    )
    return pl.pallas_call(
        functools.partial(_pre_attn_kernel, qk_norm=qk_norm),
        out_shape=out_shape,
        grid=(b, nt),
        in_specs=[
            pl.BlockSpec((None, tt, d), lambda i, t: (i, t, 0)),
            pl.BlockSpec((None, None, N_MOD, d), lambda i, t: (i, jnp.where(t == nt - 1, 0, 1), 0, 0)),
            pl.BlockSpec((1, d), lambda i, t: (0, 0)),
            pl.BlockSpec((n_qkv, d), lambda i, t: (0, 0)),
            pl.BlockSpec((HEAD_DIM, tt), lambda i, t: (0, t)),
            pl.BlockSpec((HEAD_DIM, tt), lambda i, t: (0, t)),
            pl.BlockSpec((n_qk, tt), lambda i, t: (0, 0)),
        ],
        out_specs=(
            pl.BlockSpec((None, N_KV_HEADS, None, 2 * HEAD_DIM, GQA_GROUP * tt), lambda i, t: (i, 0, t, 0, 0)),
            pl.BlockSpec((None, N_KV_HEADS // 2, tt, 2 * HEAD_DIM), lambda i, t: (i, 0, t, 0)),
            pl.BlockSpec((None, N_KV_HEADS, V_ROWS, tt), lambda i, t: (i, 0, 0, t)),
        ),
        compiler_params=_cparams(("parallel", "arbitrary"), VMEM_LIMIT),
        name="pre_attn",
    )(xs, modsel, g, wT, ct, st, gqk)


def _attn_global_kernel(q_ref, k_ref, v_ref, kc_ref, vc_ref, o_ref, m_ref, acc_ref, *, chunk):
    j = pl.program_id(3)
    n_tiles = q_ref.shape[0]
    tt = q_ref.shape[2] // GQA_GROUP
    tk = k_ref.shape[0]
    blocks = [(a, g) for a in range(n_tiles) for g in range(GQA_GROUP)]

    def cols(n):
        return slice(n * tt, (n + 1) * tt)

    def absorb(kr, vr, n_keys, step, m, acc):
        work = [(c, n) for c in range(n_keys // step) for n in range(len(blocks))]

        def scores(c, n):
            a, g = blocks[n]
            return jnp.dot(kr[c * step:(c + 1) * step, :], q_ref[a, :, cols(g)], preferred_element_type=F32)

        s_of, p_of, alpha_of = {}, {}, {}
        for t in range(len(work) + PV_LAG):
            if t < len(work):
                s_of[t] = scores(*work[t])
            if 0 <= t - SOFTMAX_LAG < len(work):
                u = t - SOFTMAX_LAG
                n = work[u][1]
                s = s_of.pop(u)
                m_new = jnp.maximum(m[n], jnp.max(s, axis=0, keepdims=True))
                p_of[u] = jnp.exp2(s - m_new).astype(BF16)
                alpha_of[u] = jnp.exp2(m[n] - m_new)
                m[n] = m_new
            if 0 <= t - PV_LAG < len(work):
                u = t - PV_LAG
                c, n = work[u]
                acc[n] = acc[n] * alpha_of.pop(u) + jnp.dot(vr[:, c * step:(c + 1) * step], p_of.pop(u),
                                                           preferred_element_type=F32)

    def save(m, acc):
        for n in range(len(blocks)):
            m_ref[:, cols(n)] = m[n]
            acc_ref[:, cols(n)] = acc[n]

    @pl.when(j == 0)
    def _():
        m0 = [jnp.full((1, tt), NEG_BIG, F32) for _ in blocks]
        acc0 = [jnp.zeros((V_ROWS, tt), F32) for _ in blocks]
        absorb(kc_ref, vc_ref, kc_ref.shape[0], kc_ref.shape[0], m0, acc0)
        save(m0, acc0)

    m = [m_ref[:, cols(n)] for n in range(len(blocks))]
    acc = [acc_ref[:, cols(n)] for n in range(len(blocks))]
    absorb(k_ref, v_ref, tk, chunk, m, acc)
    save(m, acc)

    @pl.when(j == pl.num_programs(3) - 1)
    def _():
        for n, (a, g) in enumerate(blocks):
            o = acc[n][:HEAD_DIM] / acc[n][HEAD_DIM:HEAD_DIM + 1]
            o_ref[g, :, a * tt:(a + 1) * tt] = o.astype(BF16)


def _attn_global(qT, k, vT, n_lat):
    b = qT.shape[0]
    tt = TOKEN_TILE
    n_lat_tiles = n_lat // tt
    q_tiles = 2 if n_lat_tiles % 2 == 0 else 1
    tk = 8192 if n_lat % 8192 == 0 else tt
    chunk = 512 if tk % 512 == 0 else tt
    tq = q_tiles * tt
    ctx_tile = n_lat // tt
    n_cols = q_tiles * GQA_GROUP * tt
    return pl.pallas_call(
        functools.partial(_attn_global_kernel, chunk=chunk),
        out_shape=jax.ShapeDtypeStruct((b, N_Q_HEADS, HEAD_DIM, n_lat), BF16),
        grid=(b, N_KV_HEADS, n_lat // tq, n_lat // tk),
        in_specs=[
            pl.BlockSpec((None, None, q_tiles, 2 * HEAD_DIM, GQA_GROUP * tt), lambda i, h, qi, kj: (i, h, qi, 0, 0)),
            pl.BlockSpec((None, None, tk, 2 * HEAD_DIM), lambda i, h, qi, kj: (i, h // 2, kj, 0)),
            pl.BlockSpec((None, None, V_ROWS, tk), lambda i, h, qi, kj: (i, h, 0, kj)),
            pl.BlockSpec((None, None, tt, 2 * HEAD_DIM), lambda i, h, qi, kj: (i, h // 2, ctx_tile, 0)),
            pl.BlockSpec((None, None, V_ROWS, tt), lambda i, h, qi, kj: (i, h, 0, ctx_tile)),
        ],
        out_specs=pl.BlockSpec((None, GQA_GROUP, HEAD_DIM, tq), lambda i, h, qi, kj: (i, h, 0, qi)),
        scratch_shapes=[pltpu.VMEM((1, n_cols), F32), pltpu.VMEM((V_ROWS, n_cols), F32)],
        compiler_params=_cparams(("parallel", "parallel", "parallel", "arbitrary"), VMEM_LIMIT),
        name="attn_global",
    )(qT, k, vT, k, vT)


def _attn_ctx_kernel(*refs, has_sink):
    if has_sink:
        q_ref, kc_ref, vc_ref, sink_ref, o_ref = refs
    else:
        q_ref, kc_ref, vc_ref, o_ref = refs
    tt = q_ref.shape[1] // GQA_GROUP
    s = jnp.dot(kc_ref[...], q_ref[...], preferred_element_type=F32)
    m = jnp.max(s, axis=0, keepdims=True)
    if has_sink:
        sink = sink_ref[...]
        m = jnp.maximum(m, sink)
    p = jnp.exp2(s - m).astype(BF16)
    acc = jnp.dot(vc_ref[...], p, preferred_element_type=F32)
    denom = acc[HEAD_DIM:HEAD_DIM + 1]
    if has_sink:
        denom = denom + jnp.exp2(sink - m)
    o = acc[:HEAD_DIM] / denom
    for g in range(GQA_GROUP):
        o_ref[g] = o[:, g * tt:(g + 1) * tt].astype(BF16)


def _attn_ctx(qT, k, vT, sink, n_lat):
    b = qT.shape[0]
    tt = TOKEN_TILE
    ctx_tile = n_lat // tt
    has_sink = sink is not None
    in_specs = [
        pl.BlockSpec((None, None, None, 2 * HEAD_DIM, GQA_GROUP * tt), lambda i, h: (i, h, ctx_tile, 0, 0)),
        pl.BlockSpec((None, None, tt, 2 * HEAD_DIM), lambda i, h: (i, h // 2, ctx_tile, 0)),
        pl.BlockSpec((None, None, V_ROWS, tt), lambda i, h: (i, h, 0, ctx_tile)),
    ]
    args = [qT, k, vT]
    if has_sink:
        in_specs.append(pl.BlockSpec((None, 1, GQA_GROUP * tt), lambda i, h: (h, 0, 0)))
        args.append(sink)
    return pl.pallas_call(
        functools.partial(_attn_ctx_kernel, has_sink=has_sink),
        out_shape=jax.ShapeDtypeStruct((b, N_Q_HEADS, HEAD_DIM, tt), BF16),
        grid=(b, N_KV_HEADS),
        in_specs=in_specs,
        out_specs=pl.BlockSpec((None, GQA_GROUP, HEAD_DIM, tt), lambda i, h: (i, h, 0, 0)),
        compiler_params=_cparams(("parallel", "parallel"), VMEM_LIMIT),
        name="attn_ctx",
    )(*args)


def _attn_window_kernel(q_ref, kc_ref, vc_ref, kp_ref, k_ref, kn_ref, vp_ref, v_ref, vn_ref, sink_ref, o_ref):
    i = pl.program_id(2)
    n_tiles = q_ref.shape[0]
    tt = q_ref.shape[2] // GQA_GROUP
    span = tt + 2 * WINDOW
    k_win = jnp.concatenate([kp_ref[...], k_ref[...], kn_ref[...]], axis=0)
    v_win = jnp.concatenate([vp_ref[...], v_ref[...], vn_ref[...]], axis=1)
    kk = lax.broadcasted_iota(jnp.int32, (span, tt), 0)
    t = lax.broadcasted_iota(jnp.int32, (span, tt), 1)
    in_band = (kk >= t) & (kk <= t + 2 * WINDOW)
    top = jnp.zeros((kc_ref.shape[0], tt), F32)
    keys, vals, bias = [], [], []
    for a in range(n_tiles):
        lo = jnp.where((i == 0) & (a == 0), WINDOW, 0)
        hi = jnp.where((i == pl.num_programs(2) - 1) & (a == n_tiles - 1), WINDOW + tt, span)
        ok = in_band & (kk >= lo) & (kk < hi)
        bias.append(jnp.concatenate([top, jnp.where(ok, 0.0, NEG_BIG)], axis=0))
        keys.append(jnp.concatenate([kc_ref[...], k_win[a * tt:a * tt + span]], axis=0))
        vals.append(jnp.concatenate([vc_ref[...], v_win[:, a * tt:a * tt + span]], axis=1))
    work = [(a, g) for a in range(n_tiles) for g in range(GQA_GROUP)]

    s_of, p_of, d_of = {}, {}, {}
    for step in range(len(work) + PV_LAG):
        if step < len(work):
            a, g = work[step]
            s_of[step] = jnp.dot(keys[a], q_ref[a, :, g * tt:(g + 1) * tt], preferred_element_type=F32) + bias[a]
        if 0 <= step - SOFTMAX_LAG < len(work):
            u = step - SOFTMAX_LAG
            g = work[u][1]
            sink = sink_ref[:, g * tt:(g + 1) * tt]
            s = s_of.pop(u)
            m = jnp.maximum(jnp.max(s, axis=0, keepdims=True), sink)
            p_of[u] = jnp.exp2(s - m).astype(BF16)
            d_of[u] = jnp.exp2(sink - m)
        if 0 <= step - PV_LAG < len(work):
            u = step - PV_LAG
            a, g = work[u]
            acc = jnp.dot(vals[a], p_of.pop(u), preferred_element_type=F32)
            o = acc[:HEAD_DIM] / (acc[HEAD_DIM:HEAD_DIM + 1] + d_of.pop(u))
            o_ref[g, :, a * tt:(a + 1) * tt] = o.astype(BF16)


def _attn_window(qT, k, vT, sink, n_lat):
    b = qT.shape[0]
    tt = TOKEN_TILE
    ctx_tile = n_lat // tt
    n_lat_tiles = n_lat // tt
    q_tiles = 2 if n_lat_tiles % 2 == 0 else 1
    tq = q_tiles * tt
    per_step = tq // WINDOW
    last_blk = n_lat // WINDOW - 1
    prev = lambda t: jnp.maximum(t * per_step - 1, 0)
    nxt = lambda t: jnp.minimum((t + 1) * per_step, last_blk)
    in_specs = [
        pl.BlockSpec((None, None, q_tiles, 2 * HEAD_DIM, GQA_GROUP * tt), lambda i, h, t: (i, h, t, 0, 0)),
        pl.BlockSpec((None, None, tt, 2 * HEAD_DIM), lambda i, h, t: (i, h // 2, ctx_tile, 0)),
        pl.BlockSpec((None, None, V_ROWS, tt), lambda i, h, t: (i, h, 0, ctx_tile)),
        pl.BlockSpec((None, None, WINDOW, 2 * HEAD_DIM), lambda i, h, t: (i, h // 2, prev(t), 0)),
        pl.BlockSpec((None, None, tq, 2 * HEAD_DIM), lambda i, h, t: (i, h // 2, t, 0)),
        pl.BlockSpec((None, None, WINDOW, 2 * HEAD_DIM), lambda i, h, t: (i, h // 2, nxt(t), 0)),
        pl.BlockSpec((None, None, V_ROWS, WINDOW), lambda i, h, t: (i, h, 0, prev(t))),
        pl.BlockSpec((None, None, V_ROWS, tq), lambda i, h, t: (i, h, 0, t)),
        pl.BlockSpec((None, None, V_ROWS, WINDOW), lambda i, h, t: (i, h, 0, nxt(t))),
        pl.BlockSpec((None, 1, GQA_GROUP * tt), lambda i, h, t: (h, 0, 0)),
    ]
    return pl.pallas_call(
        _attn_window_kernel,
        out_shape=jax.ShapeDtypeStruct((b, N_Q_HEADS, HEAD_DIM, n_lat), BF16),
        grid=(b, N_KV_HEADS, n_lat // tq),
        in_specs=in_specs,
        out_specs=pl.BlockSpec((None, GQA_GROUP, HEAD_DIM, tq), lambda i, h, t: (i, h, 0, t)),
        compiler_params=_cparams(("parallel", "parallel", "arbitrary"), VMEM_LIMIT),
        name="attn_window",
    )(qT, k, vT, k, k, k, vT, vT, vT, sink)


def _post_attn_kernel(ol_ref, oc_ref, x_ref, mod_ref, g_ref, woT_ref, wr_ref,
                      x1_ref, h2_ref, aff_ref, affT_ref):
    t = pl.program_id(1)
    is_ctx = t == pl.num_programs(1) - 1
    oT = jnp.where(is_ctx, oc_ref[...], ol_ref[...])
    yT = jnp.dot(woT_ref[...], oT, preferred_element_type=F32)
    x1 = x_ref[...] + mod_ref[2:3, :] * yT.T
    x1_ref[...] = x1
    ms = jnp.mean(x1 * x1, axis=-1, keepdims=True)
    h2 = x1 * lax.rsqrt(ms + NORM_EPS) * g_ref[...]
    h2 = h2 * (1.0 + mod_ref[4:5, :]) + mod_ref[3:4, :]
    h2_ref[...] = h2.astype(BF16)
    h_hi = h2.astype(BF16)
    h_lo = (h2 - h_hi.astype(F32)).astype(BF16)
    logits = (jnp.dot(h_hi, wr_ref[0], preferred_element_type=F32)
              + jnp.dot(h_lo, wr_ref[0], preferred_element_type=F32)
              + jnp.dot(h_hi, wr_ref[1], preferred_element_type=F32))
    lane = lax.broadcasted_iota(jnp.int32, logits.shape, 1)
    logits = jnp.where(lane < N_EXPERTS, logits, NEG_BIG)
    e = jnp.exp(logits - jnp.max(logits, axis=-1, keepdims=True))
    aff = e / jnp.sum(e, axis=-1, keepdims=True)
    aff_ref[...] = aff
    affT_ref[...] = aff.T[:N_EXPERTS]


def _post_attn(o_lat, o_ctx, xs, modsel, g, woT, wr):
    b, lt, d = xs.shape
    tt = TOKEN_TILE
    nt = lt // tt
    n_lat_tiles = nt - 1
    out_shape = (
        jax.ShapeDtypeStruct((b, lt, d), F32),
        jax.ShapeDtypeStruct((b, lt, d), BF16),
        jax.ShapeDtypeStruct((b, lt, LANES), F32),
        jax.ShapeDtypeStruct((b, N_EXPERTS, lt), F32),
    )
    return pl.pallas_call(
        _post_attn_kernel,
        out_shape=out_shape,
        grid=(b, nt),
        in_specs=[
            pl.BlockSpec((None, d, tt), lambda i, t: (i, 0, jnp.minimum(t, n_lat_tiles - 1))),
            pl.BlockSpec((None, d, tt), lambda i, t: (i, 0, 0)),
            pl.BlockSpec((None, tt, d), lambda i, t: (i, t, 0)),
            pl.BlockSpec((None, None, N_MOD, d), lambda i, t: (i, jnp.where(t == nt - 1, 0, 1), 0, 0)),
            pl.BlockSpec((1, d), lambda i, t: (0, 0)),
            pl.BlockSpec((d, d), lambda i, t: (0, 0)),
            pl.BlockSpec((2, d, LANES), lambda i, t: (0, 0, 0)),
        ],
        out_specs=(
            pl.BlockSpec((None, tt, d), lambda i, t: (i, t, 0)),
            pl.BlockSpec((None, tt, d), lambda i, t: (i, t, 0)),
            pl.BlockSpec((None, tt, LANES), lambda i, t: (i, t, 0)),
            pl.BlockSpec((None, N_EXPERTS, tt), lambda i, t: (i, 0, t)),
        ),
        compiler_params=_cparams(("parallel", "arbitrary"), VMEM_LIMIT),
        name="post_attn",
    )(o_lat, o_ctx, xs, modsel, g, woT, wr)


def _route_kernel(affT_ref, pos_ref, cnt_ref, *, n_lat, cap_lat, cap_ctx):
    tt = TOKEN_TILE
    lt = affT_ref.shape[1]
    ne = affT_ref.shape[0]
    r = lax.broadcasted_iota(jnp.int32, (tt, tt), 0)
    c = lax.broadcasted_iota(jnp.int32, (tt, tt), 1)
    strict_upper = jnp.where(r < c, 1.0, 0.0).astype(BF16)
    lane = lax.broadcasted_iota(jnp.int32, (ne, LANES), 1)
    cnt = jnp.zeros((ne, LANES), F32)

    def count(mask):
        return jnp.sum(jnp.where(mask, 1.0, 0.0), axis=1, keepdims=True)

    base = 0.0
    for lo, hi, cap in ((0, n_lat, cap_lat), (n_lat, lt, cap_ctx)):
        bits = pltpu.bitcast(affT_ref[:, lo:hi], jnp.int32)

        def bisect(step, thr, bits=bits, cap=cap):
            cand = thr | jnp.left_shift(jnp.int32(1), 30 - step)
            return jnp.where(count(bits >= cand) >= cap, cand, thr)

        thr = lax.fori_loop(0, 31, bisect, jnp.zeros((ne, 1), jnp.int32))
        need = cap - count(bits > thr)
        ties_before = jnp.zeros((ne, 1), F32)
        rows_before = jnp.zeros((ne, 1), F32) + base
        for ti in range((hi - lo) // tt):
            tile = lo // tt + ti
            b_t = bits[:, ti * tt:(ti + 1) * tt]
            eq = b_t == thr
            eq_f = jnp.where(eq, 1.0, 0.0)
            rank = jnp.dot(eq_f.astype(BF16), strict_upper, preferred_element_type=F32) + ties_before
            sel = (b_t > thr) | (eq & (rank < need))
            sel_f = jnp.where(sel, 1.0, 0.0)
            pos = jnp.dot(sel_f.astype(BF16), strict_upper, preferred_element_type=F32) + rows_before
            pos_ref[:, tile * tt:(tile + 1) * tt] = jnp.where(sel, pos, -1.0).astype(jnp.int32)
            cnt = jnp.where(lane == tile, rows_before, cnt)
            ties_before = ties_before + jnp.sum(eq_f, axis=1, keepdims=True)
            rows_before = rows_before + jnp.sum(sel_f, axis=1, keepdims=True)
        base = base + cap
    cnt = jnp.where(lane >= lt // tt, base, cnt)
    cnt_ref[...] = cnt.astype(jnp.int32)


def _route(affT, n_lat, cap_lat, cap_ctx):
    b, ne, lt = affT.shape
    return pl.pallas_call(
        functools.partial(_route_kernel, n_lat=n_lat, cap_lat=cap_lat, cap_ctx=cap_ctx),
        out_shape=(jax.ShapeDtypeStruct((b, ne, lt), jnp.int32),
                   jax.ShapeDtypeStruct((b, ne, LANES), jnp.int32)),
        grid=(b,),
        in_specs=[pl.BlockSpec((None, ne, lt), lambda i: (i, 0, 0))],
        out_specs=(pl.BlockSpec((None, ne, lt), lambda i: (i, 0, 0)),
                   pl.BlockSpec((None, ne, LANES), lambda i: (i, 0, 0))),
        compiler_params=_cparams(("arbitrary",), VMEM_LIMIT),
        name="route",
    )(affT)


def _gather_kernel(cnt_ref, pos_ref, h_ref, o_ref, *, cap):
    tt = TOKEN_TILE
    i, eg, s = pl.program_id(0), pl.program_id(1), pl.program_id(2)
    ng = o_ref.shape[0]
    sub = h_ref.shape[0] // tt
    last_start = cap - ROW_WINDOW

    @pl.when(s == 0)
    def _():
        o_ref[...] = jnp.zeros(o_ref.shape, BF16)

    for u in range(sub):
        r0, n_win = [], 0
        for k in range(ng):
            base = (i * N_EXPERTS + eg * ng + k) * LANES + s * sub + u
            c0, c1 = cnt_ref[base], cnt_ref[base + 1]
            r0.append((c0 // 16) * 16)
            n_win = jnp.maximum(n_win, jnp.where(c1 > c0, (c1 - r0[k] + ROW_WINDOW - 1) // ROW_WINDOW, 0))
        hb = h_ref[u * tt:(u + 1) * tt, :]

        def window(w, carry, r0=r0, hb=hb, u=u):
            starts, pieces = [], []
            for k in range(ng):
                want = r0[k] + w * ROW_WINDOW
                rs = pl.multiple_of(jnp.minimum(want, last_start), 16)
                rows = rs + lax.broadcasted_iota(jnp.int32, (ROW_WINDOW, tt), 0)
                hit = (pos_ref[k, :, u * tt:(u + 1) * tt] == rows) & (rows >= want)
                pieces.append(jnp.where(hit, 1.0, 0.0).astype(BF16))
                starts.append(rs)
            got = jnp.dot(jnp.concatenate(pieces, axis=0), hb, preferred_element_type=F32)
            for k in range(ng):
                o_ref[k, pl.ds(starts[k], ROW_WINDOW), :] += got[k * ROW_WINDOW:(k + 1) * ROW_WINDOW].astype(BF16)
            return carry

        lax.fori_loop(0, n_win, window, 0)


def _gather_tokens_per_step(nt):
    for k in (5, 4, 2, 1):
        if nt % k == 0:
            return k * TOKEN_TILE


def _gather(cnt_flat, pos4, h2, cap):
    b, lt, d = h2.shape
    gt = _gather_tokens_per_step(lt // TOKEN_TILE)
    ng = GATHER_EXPERT_GROUP
    grid_spec = pltpu.PrefetchScalarGridSpec(
        num_scalar_prefetch=1,
        grid=(b, N_EXPERTS // ng, lt // gt),
        in_specs=[pl.BlockSpec((None, ng, 1, gt), lambda i, e, s, cnt: (i, e, 0, s)),
                  pl.BlockSpec((None, gt, d), lambda i, e, s, cnt: (i, s, 0))],
        out_specs=pl.BlockSpec((None, ng, cap, d), lambda i, e, s, cnt: (i, e, 0, 0)),
    )
    return pl.pallas_call(
        functools.partial(_gather_kernel, cap=cap),
        out_shape=jax.ShapeDtypeStruct((b, N_EXPERTS, cap, d), BF16),
        grid_spec=grid_spec,
        compiler_params=_cparams(("parallel", "parallel", "arbitrary"), VMEM_LIMIT),
        name="moe_gather",
    )(cnt_flat, pos4, h2)


def _ffn_kernel(x_ref, wg_ref, wu_ref, wd_ref, y_ref):
    x = x_ref[...]
    acc = jnp.zeros((x.shape[0], wd_ref.shape[1]), F32)
    for f in range(wg_ref.shape[1] // FFN_SLICE):
        sl = slice(f * FFN_SLICE, (f + 1) * FFN_SLICE)
        g = jnp.dot(x, wg_ref[:, sl], preferred_element_type=F32)
        u = jnp.dot(x, wu_ref[:, sl], preferred_element_type=F32)
        a = (g * (1.0 / (1.0 + jnp.exp(-g))) * u).astype(BF16)
        acc = acc + jnp.dot(a, wd_ref[sl, :], preferred_element_type=F32)
    y_ref[...] = acc.astype(BF16)


def _ffn(x_sel, wg, wu, wd, layer):
    b, ne, cap, d = x_sel.shape
    f = wg.shape[3]
    n_row_tiles = 2 if cap % 32 == 0 else 1
    rt = cap // n_row_tiles
    return pl.pallas_call(
        _ffn_kernel,
        out_shape=jax.ShapeDtypeStruct((b, ne, cap, d), BF16),
        grid=(ne, b * n_row_tiles),
        in_specs=[
            pl.BlockSpec((None, None, rt, d), lambda e, r: (r // n_row_tiles, e, r % n_row_tiles, 0)),
            pl.BlockSpec((None, None, d, f), lambda e, r: (layer, e, 0, 0)),
            pl.BlockSpec((None, None, d, f), lambda e, r: (layer, e, 0, 0)),
            pl.BlockSpec((None, None, f, d), lambda e, r: (layer, e, 0, 0)),
        ],
        out_specs=pl.BlockSpec((None, None, rt, d), lambda e, r: (r // n_row_tiles, e, r % n_row_tiles, 0)),
        compiler_params=_cparams(("parallel", "arbitrary"), VMEM_LIMIT),
        name="moe_ffn",
    )(x_sel, wg, wu, wd)


def _combine_kernel(cnt_ref, x_ref, pos_ref, aff_ref, mod_ref, *rest, cap, final):
    if final:
        gfin_ref, y_hbm, o_ref, buf_ref, xbuf_ref, acc_ref, sem_ref, xsem_ref = rest
    else:
        y_hbm, o_ref, buf_ref, xbuf_ref, acc_ref, sem_ref, xsem_ref = rest
    tt = TOKEN_TILE
    i, t = pl.program_id(0), pl.program_id(1)
    n_t = pl.num_programs(1)
    ne = pos_ref.shape[0]
    last_start = cap - ROW_WINDOW
    step = i * n_t + t
    slot = step % 2

    def first_window(bi, ti, e):
        c0 = cnt_ref[(bi * ne + e) * LANES + ti]
        return pl.multiple_of(jnp.minimum((c0 // 16) * 16, last_start), 16)

    def window_copy(bi, ti, e, sl):
        rs = first_window(bi, ti, e)
        return pltpu.make_async_copy(y_hbm.at[bi, e, pl.ds(rs, ROW_WINDOW), :], buf_ref.at[sl, e], sem_ref.at[sl, e])

    @pl.when(step == 0)
    def _():
        for e in range(ne):
            window_copy(i, t, e, slot).start()

    @pl.when(step + 1 < pl.num_programs(0) * n_t)
    def _():
        wrap = t + 1 == n_t
        for e in range(ne):
            window_copy(jnp.where(wrap, i + 1, i), jnp.where(wrap, 0, t + 1), e, 1 - slot).start()

    pos_f = pos_ref[...].astype(F32)
    pad = jnp.full((LANES - ne, tt), -1.0, F32)
    posT = jnp.concatenate([pos_f, pad], axis=0).T
    lane = lax.broadcasted_iota(jnp.int32, (1, ROW_WINDOW), 1).astype(F32)

    hi, lo = [], []
    for e in range(ne):
        hit = posT[:, e:e + 1] == first_window(i, t, e).astype(F32) + lane
        gated = jnp.where(hit, aff_ref[:, e:e + 1], 0.0)
        g_hi = gated.astype(BF16)
        hi.append(g_hi)
        lo.append((gated - g_hi.astype(F32)).astype(BF16))
    for e in range(ne):
        window_copy(i, t, e, slot).wait()
    rows = buf_ref[slot].reshape(ne * ROW_WINDOW, buf_ref.shape[-1])
    acc_ref[...] = (jnp.dot(jnp.concatenate(hi, axis=1), rows, preferred_element_type=F32)
                    + jnp.dot(jnp.concatenate(lo, axis=1), rows, preferred_element_type=F32))

    for e in range(ne):
        rs = first_window(i, t, e)
        c1 = cnt_ref[(i * ne + e) * LANES + t + 1]
        pcol = posT[:, e:e + 1]
        gate = aff_ref[:, e:e + 1]
        n_extra = jnp.maximum(c1 - rs - 1, 0) // ROW_WINDOW

        def extra(w, carry, e=e, rs=rs, pcol=pcol, gate=gate):
            want = rs + (w + 1) * ROW_WINDOW
            start = pl.multiple_of(jnp.minimum(want, last_start), 16)
            cp = pltpu.make_async_copy(y_hbm.at[i, e, pl.ds(start, ROW_WINDOW), :], xbuf_ref, xsem_ref)
            cp.start()
            cp.wait()
            row = start.astype(F32) + lane
            hit = (pcol == row) & (row >= want.astype(F32))
            acc_ref[...] += gate * jnp.dot(jnp.where(hit, 1.0, 0.0).astype(BF16), xbuf_ref[...],
                                           preferred_element_type=F32)
            return carry

        lax.fori_loop(0, n_extra, extra, 0)

    x2 = x_ref[...] + mod_ref[5:6, :] * acc_ref[...]
    if final:
        ms = jnp.mean(x2 * x2, axis=-1, keepdims=True)
        x2 = x2 * lax.rsqrt(ms + NORM_EPS) * gfin_ref[...]
    o_ref[...] = x2


def _combine(cnt_flat, x1, pos, aff, modsel, y, n_lat, final_g=None):
    b, lt, d = x1.shape
    tt = TOKEN_TILE
    nt = lt // tt
    cap = y.shape[2]
    final = final_g is not None
    n_steps = n_lat // tt if final else nt
    n_out = n_lat if final else lt
    in_specs = [
        pl.BlockSpec((None, tt, d), lambda i, t, cnt: (i, t, 0)),
        pl.BlockSpec((None, N_EXPERTS, tt), lambda i, t, cnt: (i, 0, t)),
        pl.BlockSpec((None, tt, LANES), lambda i, t, cnt: (i, t, 0)),
        pl.BlockSpec((None, None, N_MOD, d), lambda i, t, cnt: (i, jnp.where(t == nt - 1, 0, 1), 0, 0)),
    ]
    args = [x1, pos, aff, modsel]
    if final:
        in_specs.append(pl.BlockSpec((1, d), lambda i, t, cnt: (0, 0)))
        args.append(final_g)
    in_specs.append(pl.BlockSpec(memory_space=pl.ANY))
    args.append(y)
    grid_spec = pltpu.PrefetchScalarGridSpec(
        num_scalar_prefetch=1,
        grid=(b, n_steps),
        in_specs=in_specs,
        out_specs=pl.BlockSpec((None, tt, d), lambda i, t, cnt: (i, t, 0)),
        scratch_shapes=[
            pltpu.VMEM((2, N_EXPERTS, ROW_WINDOW, d), BF16),
            pltpu.VMEM((ROW_WINDOW, d), BF16),
            pltpu.VMEM((tt, d), F32),
            pltpu.SemaphoreType.DMA((2, N_EXPERTS)),
            pltpu.SemaphoreType.DMA(()),
        ],
    )
    return pl.pallas_call(
        functools.partial(_combine_kernel, cap=cap, final=final),
        out_shape=jax.ShapeDtypeStruct((b, n_out, d), F32),
        grid_spec=grid_spec,
        compiler_params=_cparams(("arbitrary", "arbitrary"), VMEM_LIMIT),
        name="moe_combine",
    )(cnt_flat, *args)


def _rope_tables(n_lat, n_ctx):
    rows = n_lat // GRID_W
    row = jnp.repeat(jnp.arange(rows, dtype=F32), GRID_W)
    col = jnp.tile(jnp.arange(GRID_W, dtype=F32), rows)
    inv_freq = 1.0 / (ROPE_THETA ** (jnp.arange(0, ROPE_AXIS_DIM, 2, dtype=F32) / ROPE_AXIS_DIM))
    ang_r = (row[:, None] * inv_freq).T
    ang_c = (col[:, None] * inv_freq).T
    ct = jnp.concatenate([jnp.cos(ang_r), jnp.cos(ang_r), jnp.cos(ang_c), jnp.cos(ang_c)], axis=0)
    st = jnp.concatenate([-jnp.sin(ang_r), jnp.sin(ang_r), -jnp.sin(ang_c), jnp.sin(ang_c)], axis=0)
    ct = jnp.concatenate([ct, jnp.ones((HEAD_DIM, n_ctx), F32)], axis=1)
    st = jnp.concatenate([st, jnp.zeros((HEAD_DIM, n_ctx), F32)], axis=1)
    return ct, st


def kernel(x, c, ctx, c_ctx, w_mod, b_mod, norm_mix_g, norm_ffn_g, w_qkv, w_o, q_norm_g, k_norm_g,
           attn_sink, w_router, w_gate, w_up, w_down, final_norm_g):
    b, n_lat, d = x.shape
    n_ctx = ctx.shape[1]
    depth = w_mod.shape[0]
    tt = TOKEN_TILE
    assert n_ctx == tt and n_lat % tt == 0 and d == N_Q_HEADS * HEAD_DIM
    cap_lat = max(1, EC_CAPACITY * n_lat // N_EXPERTS)
    cap_ctx = max(1, EC_CAPACITY * n_ctx // N_EXPERTS)
    cap = cap_lat + cap_ctx
    assert cap % 16 == 0 and cap >= ROW_WINDOW and (cap - ROW_WINDOW) % 16 == 0

    cond = jnp.concatenate([c, c_ctx[None, :], jnp.zeros((8 - b - 1, d), F32)], axis=0)
    mod = _modulation(cond, w_mod, b_mod)
    mod = mod.reshape(depth, 8, N_MOD, d)
    modsel = jnp.stack([jnp.broadcast_to(mod[:, b][:, None], (depth, b, N_MOD, d)), mod[:, :b]], axis=2)

    ct, st = _rope_tables(n_lat, n_ctx)
    xs = jnp.concatenate([x, ctx], axis=1)
    wg, wu, wd = w_gate.astype(BF16), w_up.astype(BF16), w_down.astype(BF16)
    n_qk =(N_Q_HEADS + N_KV_HEADS) * HEAD_DIM

    for i in range(depth):
        last = i == depth - 1
        is_global = i % 2 == 0
        j = i // 2
        wT = w_qkv[i].T.astype(BF16)
        if is_global:
            gqk = jnp.concatenate([jnp.tile(q_norm_g[j], N_Q_HEADS), jnp.tile(k_norm_g[j], N_KV_HEADS)])
        else:
            gqk = jnp.ones((n_qk,), F32)
        gqk = jnp.broadcast_to(gqk[:, None], (n_qk, tt))
        qT, k, vT = _pre_attn(xs, modsel[i], norm_mix_g[i][None, :], wT, ct, st, gqk, is_global)
        if is_global:
            o_lat = _attn_global(qT, k, vT, n_lat)
            o_ctx = _attn_ctx(qT, k, vT, None, n_lat)
        else:
            sink = jnp.repeat(attn_sink[j].reshape(N_KV_HEADS, GQA_GROUP) * LOG2_E, tt, axis=1)[:, None, :]
            o_lat = _attn_window(qT, k, vT, sink, n_lat)
            o_ctx = _attn_ctx(qT, k, vT, sink, n_lat)
        wr = jnp.pad(w_router[i], ((0, 0), (0, LANES - N_EXPERTS)))
        wr_hi = wr.astype(BF16)
        wr = jnp.stack([wr_hi, (wr - wr_hi.astype(F32)).astype(BF16)])
        x1, h2, aff, affT = _post_attn(o_lat.reshape(b, d, n_lat), o_ctx.reshape(b, d, n_ctx), xs, modsel[i],
                                       norm_ffn_g[i][None, :], w_o[i].T.astype(BF16), wr)
        pos, cnt = _route(affT, n_lat, cap_lat, cap_ctx)
        cnt_flat = cnt.reshape(-1)
        x_sel = _gather(cnt_flat, pos.reshape(b, N_EXPERTS, 1, n_lat + n_ctx), h2, cap)
        y = _ffn(x_sel, wg, wu, wd, i)
        xs = _combine(cnt_flat, x1, pos, aff, modsel[i], y, n_lat,
                      final_g=final_norm_g[None, :] if last else None)
    return xs
```

```python
import functools

import jax
import jax.numpy as jnp
from jax import lax
from jax.experimental import pallas as pl
from jax.experimental.pallas import tpu as pltpu

HEAD_DIM = 64
N_KV_HEADS = 4
GQA_GROUP = 4
N_Q_HEADS = N_KV_HEADS * GQA_GROUP
ROPE_AXIS_DIM = HEAD_DIM // 2
ROPE_HALF = ROPE_AXIS_DIM // 2
ROPE_THETA = 10000.0
GRID_W = 64
WINDOW = 128
N_EXPERTS = 16
EC_CAPACITY = 2
N_MOD = 6
NORM_EPS = 1e-6
NEG_BIG = -1e30
LOG2_E = 1.4426950408889634
Q_SCALE = HEAD_DIM ** -0.5 * LOG2_E

TOKEN_TILE = 256
LANES = 128
V_ROWS = 80
ROW_WINDOW = 128
FFN_SLICE = 512
GATHER_EXPERT_GROUP = 4
SOFTMAX_LAG = 1
PV_LAG = 3
VMEM_LIMIT = 56 * 1024 * 1024

F32 = jnp.float32
BF16 = jnp.bfloat16


def _cparams(sem, vmem=None):
    return pltpu.CompilerParams(dimension_semantics=sem, vmem_limit_bytes=vmem)


def _mod_kernel(c_ref, w_ref, b_ref, o_ref):
    c = c_ref[...]
    s = c * (1.0 / (1.0 + jnp.exp(-c)))
    o_ref[...] = jnp.dot(s, w_ref[...], preferred_element_type=F32,
                         precision=lax.Precision.HIGHEST) + b_ref[...]


def _modulation(cond, w_mod, b_mod):
    depth, d, n = w_mod.shape
    nb = n // 4
    return pl.pallas_call(
        _mod_kernel,
        out_shape=jax.ShapeDtypeStruct((depth, 8, n), F32),
        grid=(depth, n // nb),
        in_specs=[pl.BlockSpec((8, d), lambda l, j: (0, 0)),
                  pl.BlockSpec((None, d, nb), lambda l, j: (l, 0, j)),
                  pl.BlockSpec((None, 1, nb), lambda l, j: (l, 0, j))],
        out_specs=pl.BlockSpec((None, 8, nb), lambda l, j: (l, 0, j)),
        compiler_params=_cparams(("arbitrary", "arbitrary"), VMEM_LIMIT),
        name="modulation",
    )(cond, w_mod, b_mod.reshape(depth, 1, n))


def _pre_attn_kernel(x_ref, mod_ref, g_ref, wT_ref, ct_ref, st_ref, gqk_ref,
                     q_ref, k_ref, v_ref, *, qk_norm):
    tt = x_ref.shape[0]
    x = x_ref[...]
    ms = jnp.mean(x * x, axis=-1, keepdims=True)
    h = x * lax.rsqrt(ms + NORM_EPS) * g_ref[...]
    h = h * (1.0 + mod_ref[1:2, :]) + mod_ref[0:1, :]
    qkv = lax.dot_general(wT_ref[...], h.astype(BF16), (((1,), (1,)), ((), ())),
                          preferred_element_type=F32)
    n_qk = (N_Q_HEADS + N_KV_HEADS) * HEAD_DIM
    n_heads = N_Q_HEADS + N_KV_HEADS
    qk = qkv[:n_qk].reshape(n_heads, HEAD_DIM, tt)
    if qk_norm:
        ss = jnp.mean(qk * qk, axis=1, keepdims=True)
        qk = qk * lax.rsqrt(ss + NORM_EPS) * gqk_ref[...].reshape(n_heads, HEAD_DIM, tt)
    q5 = qk.reshape(n_heads, 2, 2, ROPE_HALF, tt)
    swapped = jnp.concatenate([q5[:, :, 1:2], q5[:, :, 0:1]], axis=2).reshape(n_heads, HEAD_DIM, tt)
    qk = qk * ct_ref[...][None] + swapped * st_ref[...][None]

    zeros64 = jnp.zeros((HEAD_DIM, tt), BF16)
    for hh in range(N_KV_HEADS):
        for g in range(GQA_GROUP):
            piece = (qk[hh * GQA_GROUP + g] * Q_SCALE).astype(BF16)
            pair = [piece, zeros64] if hh % 2 == 0 else [zeros64, piece]
            q_ref[hh, :, g * tt:(g + 1) * tt] = jnp.concatenate(pair, axis=0)
    kT = qk[N_Q_HEADS:].reshape(N_KV_HEADS * HEAD_DIM, tt)
    for p in range(N_KV_HEADS // 2):
        k_ref[p] = kT[p * LANES:(p + 1) * LANES].T.astype(BF16)
    v = qkv[n_qk:]
    row = lax.broadcasted_iota(jnp.int32, (V_ROWS - HEAD_DIM, tt), 0)
    tail = jnp.where(row == 0, 1.0, 0.0).astype(BF16)
    for hh in range(N_KV_HEADS):
        v_ref[hh] = jnp.concatenate([v[hh * HEAD_DIM:(hh + 1) * HEAD_DIM].astype(BF16), tail], axis=0)


def _pre_attn(xs, modsel, g, wT, ct, st, gqk, qk_norm):
    b, lt, d = xs.shape
    tt = TOKEN_TILE
    nt = lt // tt
    n_qkv = wT.shape[0]
    n_qk = gqk.shape[0]
    out_shape = (
        jax.ShapeDtypeStruct((b, N_KV_HEADS, nt, 2 * HEAD_DIM, GQA_GROUP * tt), BF16),
        jax.ShapeDtypeStruct((b, N_KV_HEADS // 2, lt, 2 * HEAD_DIM), BF16),
        jax.ShapeDtypeStruct((b, N_KV_HEADS, V_ROWS, lt), BF16),
    )
    return pl.pallas_call(
        functools.partial(_pre_attn_kernel, qk_norm=qk_norm),
        out_shape=out_shape,
        grid=(b, nt),
        in_specs=[
            pl.BlockSpec((None, tt, d), lambda i, t: (i, t, 0)),
            pl.BlockSpec((None, None, N_MOD, d), lambda i, t: (i, jnp.where(t == nt - 1, 0, 1), 0, 0)),
            pl.BlockSpec((1, d), lambda i, t: (0, 0)),
            pl.BlockSpec((n_qkv, d), lambda i, t: (0, 0)),
            pl.BlockSpec((HEAD_DIM, tt), lambda i, t: (0, t)),
            pl.BlockSpec((HEAD_DIM, tt), lambda i, t: (0, t)),
            pl.BlockSpec((n_qk, tt), lambda i, t: (0, 0)),
        ],
        out_specs=(
            pl.BlockSpec((None, N_KV_HEADS, None, 2 * HEAD_DIM, GQA_GROUP * tt), lambda i, t: (i, 0, t, 0, 0)),
            pl.BlockSpec((None, N_KV_HEADS // 2, tt, 2 * HEAD_DIM), lambda i, t: (i, 0, t, 0)),
            pl.BlockSpec((None, N_KV_HEADS, V_ROWS, tt), lambda i, t: (i, 0, 0, t)),
        ),
        compiler_params=_cparams(("parallel", "arbitrary"), VMEM_LIMIT),
        name="pre_attn",
    )(xs, modsel, g, wT, ct, st, gqk)


def _attn_global_kernel(q_ref, k_ref, v_ref, kc_ref, vc_ref, o_ref, m_ref, acc_ref, *, chunk):
    j = pl.program_id(3)
    n_tiles = q_ref.shape[0]
    tt = q_ref.shape[2] // GQA_GROUP
    tk = k_ref.shape[0]
    blocks = [(a, g) for a in range(n_tiles) for g in range(GQA_GROUP)]

    def cols(n):
        return slice(n * tt, (n + 1) * tt)

    def absorb(kr, vr, n_keys, step, m, acc):
        work = [(c, n) for c in range(n_keys // step) for n in range(len(blocks))]

        def scores(c, n):
            a, g = blocks[n]
            return jnp.dot(kr[c * step:(c + 1) * step, :], q_ref[a, :, cols(g)], preferred_element_type=F32)

        s_of, p_of, alpha_of = {}, {}, {}
        for t in range(len(work) + PV_LAG):
            if t < len(work):
                s_of[t] = scores(*work[t])
            if 0 <= t - SOFTMAX_LAG < len(work):
                u = t - SOFTMAX_LAG
                n = work[u][1]
                s = s_of.pop(u)
                m_new = jnp.maximum(m[n], jnp.max(s, axis=0, keepdims=True))
                p_of[u] = jnp.exp2(s - m_new).astype(BF16)
                alpha_of[u] = jnp.exp2(m[n] - m_new)
                m[n] = m_new
            if 0 <= t - PV_LAG < len(work):
                u = t - PV_LAG
                c, n = work[u]
                acc[n] = acc[n] * alpha_of.pop(u) + jnp.dot(vr[:, c * step:(c + 1) * step], p_of.pop(u),
                                                           preferred_element_type=F32)

    def save(m, acc):
        for n in range(len(blocks)):
            m_ref[:, cols(n)] = m[n]
            acc_ref[:, cols(n)] = acc[n]

    @pl.when(j == 0)
    def _():
        m0 = [jnp.full((1, tt), NEG_BIG, F32) for _ in blocks]
        acc0 = [jnp.zeros((V_ROWS, tt), F32) for _ in blocks]
        absorb(kc_ref, vc_ref, kc_ref.shape[0], kc_ref.shape[0], m0, acc0)
        save(m0, acc0)

    m = [m_ref[:, cols(n)] for n in range(len(blocks))]
    acc = [acc_ref[:, cols(n)] for n in range(len(blocks))]
    absorb(k_ref, v_ref, tk, chunk, m, acc)
    save(m, acc)

    @pl.when(j == pl.num_programs(3) - 1)
    def _():
        for n, (a, g) in enumerate(blocks):
            o = acc[n][:HEAD_DIM] / acc[n][HEAD_DIM:HEAD_DIM + 1]
            o_ref[g, :, a * tt:(a + 1) * tt] = o.astype(BF16)


def _attn_global(qT, k, vT, n_lat):
    b = qT.shape[0]
    tt = TOKEN_TILE
    n_lat_tiles = n_lat // tt
    q_tiles = 2 if n_lat_tiles % 2 == 0 else 1
    tk = 8192 if n_lat % 8192 == 0 else tt
    chunk = 512 if tk % 512 == 0 else tt
    tq = q_tiles * tt
    ctx_tile = n_lat // tt
    n_cols = q_tiles * GQA_GROUP * tt
    return pl.pallas_call(
        functools.partial(_attn_global_kernel, chunk=chunk),
        out_shape=jax.ShapeDtypeStruct((b, N_Q_HEADS, HEAD_DIM, n_lat), BF16),
        grid=(b, N_KV_HEADS, n_lat // tq, n_lat // tk),
        in_specs=[
            pl.BlockSpec((None, None, q_tiles, 2 * HEAD_DIM, GQA_GROUP * tt), lambda i, h, qi, kj: (i, h, qi, 0, 0)),
            pl.BlockSpec((None, None, tk, 2 * HEAD_DIM), lambda i, h, qi, kj: (i, h // 2, kj, 0)),
            pl.BlockSpec((None, None, V_ROWS, tk), lambda i, h, qi, kj: (i, h, 0, kj)),
            pl.BlockSpec((None, None, tt, 2 * HEAD_DIM), lambda i, h, qi, kj: (i, h // 2, ctx_tile, 0)),
            pl.BlockSpec((None, None, V_ROWS, tt), lambda i, h, qi, kj: (i, h, 0, ctx_tile)),
        ],
        out_specs=pl.BlockSpec((None, GQA_GROUP, HEAD_DIM, tq), lambda i, h, qi, kj: (i, h, 0, qi)),
        scratch_shapes=[pltpu.VMEM((1, n_cols), F32), pltpu.VMEM((V_ROWS, n_cols), F32)],
        compiler_params=_cparams(("parallel", "parallel", "parallel", "arbitrary"), VMEM_LIMIT),
        name="attn_global",
    )(qT, k, vT, k, vT)


def _attn_ctx_kernel(*refs, has_sink):
    if has_sink:
        q_ref, kc_ref, vc_ref, sink_ref, o_ref = refs
    else:
        q_ref, kc_ref, vc_ref, o_ref = refs
    tt = q_ref.shape[1] // GQA_GROUP
    s = jnp.dot(kc_ref[...], q_ref[...], preferred_element_type=F32)
    m = jnp.max(s, axis=0, keepdims=True)
    if has_sink:
        sink = sink_ref[...]
        m = jnp.maximum(m, sink)
    p = jnp.exp2(s - m).astype(BF16)
    acc = jnp.dot(vc_ref[...], p, preferred_element_type=F32)
    denom = acc[HEAD_DIM:HEAD_DIM + 1]
    if has_sink:
        denom = denom + jnp.exp2(sink - m)
    o = acc[:HEAD_DIM] / denom
    for g in range(GQA_GROUP):
        o_ref[g] = o[:, g * tt:(g + 1) * tt].astype(BF16)


def _attn_ctx(qT, k, vT, sink, n_lat):
    b = qT.shape[0]
    tt = TOKEN_TILE
    ctx_tile = n_lat // tt
    has_sink = sink is not None
    in_specs = [
        pl.BlockSpec((None, None, None, 2 * HEAD_DIM, GQA_GROUP * tt), lambda i, h: (i, h, ctx_tile, 0, 0)),
        pl.BlockSpec((None, None, tt, 2 * HEAD_DIM), lambda i, h: (i, h // 2, ctx_tile, 0)),
        pl.BlockSpec((None, None, V_ROWS, tt), lambda i, h: (i, h, 0, ctx_tile)),
    ]
    args = [qT, k, vT]
    if has_sink:
        in_specs.append(pl.BlockSpec((None, 1, GQA_GROUP * tt), lambda i, h: (h, 0, 0)))
        args.append(sink)
    return pl.pallas_call(
        functools.partial(_attn_ctx_kernel, has_sink=has_sink),
        out_shape=jax.ShapeDtypeStruct((b, N_Q_HEADS, HEAD_DIM, tt), BF16),
        grid=(b, N_KV_HEADS),
        in_specs=in_specs,
        out_specs=pl.BlockSpec((None, GQA_GROUP, HEAD_DIM, tt), lambda i, h: (i, h, 0, 0)),
        compiler_params=_cparams(("parallel", "parallel"), VMEM_LIMIT),
        name="attn_ctx",
    )(*args)


def _attn_window_kernel(q_ref, kc_ref, vc_ref, kp_ref, k_ref, kn_ref, vp_ref, v_ref, vn_ref, sink_ref, o_ref):
    i = pl.program_id(2)
    n_tiles = q_ref.shape[0]
    tt = q_ref.shape[2] // GQA_GROUP
    span = tt + 2 * WINDOW
    k_win = jnp.concatenate([kp_ref[...], k_ref[...], kn_ref[...]], axis=0)
    v_win = jnp.concatenate([vp_ref[...], v_ref[...], vn_ref[...]], axis=1)
    kk = lax.broadcasted_iota(jnp.int32, (span, tt), 0)
    t = lax.broadcasted_iota(jnp.int32, (span, tt), 1)
    in_band = (kk >= t) & (kk <= t + 2 * WINDOW)
    top = jnp.zeros((kc_ref.shape[0], tt), F32)
    keys, vals, bias = [], [], []
    for a in range(n_tiles):
        lo = jnp.where((i == 0) & (a == 0), WINDOW, 0)
        hi = jnp.where((i == pl.num_programs(2) - 1) & (a == n_tiles - 1), WINDOW + tt, span)
        ok = in_band & (kk >= lo) & (kk < hi)
        bias.append(jnp.concatenate([top, jnp.where(ok, 0.0, NEG_BIG)], axis=0))
        keys.append(jnp.concatenate([kc_ref[...], k_win[a * tt:a * tt + span]], axis=0))
        vals.append(jnp.concatenate([vc_ref[...], v_win[:, a * tt:a * tt + span]], axis=1))
    work = [(a, g) for a in range(n_tiles) for g in range(GQA_GROUP)]

    s_of, p_of, d_of = {}, {}, {}
    for step in range(len(work) + PV_LAG):
        if step < len(work):
            a, g = work[step]
            s_of[step] = jnp.dot(keys[a], q_ref[a, :, g * tt:(g + 1) * tt], preferred_element_type=F32) + bias[a]
        if 0 <= step - SOFTMAX_LAG < len(work):
            u = step - SOFTMAX_LAG
            g = work[u][1]
            sink = sink_ref[:, g * tt:(g + 1) * tt]
            s = s_of.pop(u)
            m = jnp.maximum(jnp.max(s, axis=0, keepdims=True), sink)
            p_of[u] = jnp.exp2(s - m).astype(BF16)
            d_of[u] = jnp.exp2(sink - m)
        if 0 <= step - PV_LAG < len(work):
            u = step - PV_LAG
            a, g = work[u]
            acc = jnp.dot(vals[a], p_of.pop(u), preferred_element_type=F32)
            o = acc[:HEAD_DIM] / (acc[HEAD_DIM:HEAD_DIM + 1] + d_of.pop(u))
            o_ref[g, :, a * tt:(a + 1) * tt] = o.astype(BF16)


def _attn_window(qT, k, vT, sink, n_lat):
    b = qT.shape[0]
    tt = TOKEN_TILE
    ctx_tile = n_lat // tt
    n_lat_tiles = n_lat // tt
    q_tiles = next(n for n in (4, 2, 1) if n_lat_tiles % n == 0)
    tq = q_tiles * tt
    per_step = tq // WINDOW
    last_blk = n_lat // WINDOW - 1
    prev = lambda t: jnp.maximum(t * per_step - 1, 0)
    nxt = lambda t: jnp.minimum((t + 1) * per_step, last_blk)
    in_specs = [
        pl.BlockSpec((None, None, q_tiles, 2 * HEAD_DIM, GQA_GROUP * tt), lambda i, h, t: (i, h, t, 0, 0)),
        pl.BlockSpec((None, None, tt, 2 * HEAD_DIM), lambda i, h, t: (i, h // 2, ctx_tile, 0)),
        pl.BlockSpec((None, None, V_ROWS, tt), lambda i, h, t: (i, h, 0, ctx_tile)),
        pl.BlockSpec((None, None, WINDOW, 2 * HEAD_DIM), lambda i, h, t: (i, h // 2, prev(t), 0)),
        pl.BlockSpec((None, None, tq, 2 * HEAD_DIM), lambda i, h, t: (i, h // 2, t, 0)),
        pl.BlockSpec((None, None, WINDOW, 2 * HEAD_DIM), lambda i, h, t: (i, h // 2, nxt(t), 0)),
        pl.BlockSpec((None, None, V_ROWS, WINDOW), lambda i, h, t: (i, h, 0, prev(t))),
        pl.BlockSpec((None, None, V_ROWS, tq), lambda i, h, t: (i, h, 0, t)),
        pl.BlockSpec((None, None, V_ROWS, WINDOW), lambda i, h, t: (i, h, 0, nxt(t))),
        pl.BlockSpec((None, 1, GQA_GROUP * tt), lambda i, h, t: (h, 0, 0)),
    ]
    return pl.pallas_call(
        _attn_window_kernel,
        out_shape=jax.ShapeDtypeStruct((b, N_Q_HEADS, HEAD_DIM, n_lat), BF16),
        grid=(b, N_KV_HEADS, n_lat // tq),
        in_specs=in_specs,
        out_specs=pl.BlockSpec((None, GQA_GROUP, HEAD_DIM, tq), lambda i, h, t: (i, h, 0, t)),
        compiler_params=_cparams(("parallel", "parallel", "arbitrary"), VMEM_LIMIT),
        name="attn_window",
    )(qT, k, vT, k, k, k, vT, vT, vT, sink)


def _post_attn_kernel(ol_ref, oc_ref, x_ref, mod_ref, g_ref, woT_ref, wr_ref,
                      x1_ref, h2_ref, aff_ref, affT_ref):
    t = pl.program_id(1)
    is_ctx = t == pl.num_programs(1) - 1
    oT = jnp.where(is_ctx, oc_ref[...], ol_ref[...])
    yT = jnp.dot(woT_ref[...], oT, preferred_element_type=F32)
    x1 = x_ref[...] + mod_ref[2:3, :] * yT.T
    x1_ref[...] = x1
    ms = jnp.mean(x1 * x1, axis=-1, keepdims=True)
    h2 = x1 * lax.rsqrt(ms + NORM_EPS) * g_ref[...]
    h2 = h2 * (1.0 + mod_ref[4:5, :]) + mod_ref[3:4, :]
    h2_ref[...] = h2.astype(BF16)
    h_hi = h2.astype(BF16)
    h_lo = (h2 - h_hi.astype(F32)).astype(BF16)
    logits = (jnp.dot(h_hi, wr_ref[0], preferred_element_type=F32)
              + jnp.dot(h_lo, wr_ref[0], preferred_element_type=F32)
              + jnp.dot(h_hi, wr_ref[1], preferred_element_type=F32))
    lane = lax.broadcasted_iota(jnp.int32, logits.shape, 1)
    logits = jnp.where(lane < N_EXPERTS, logits, NEG_BIG)
    e = jnp.exp(logits - jnp.max(logits, axis=-1, keepdims=True))
    aff = e / jnp.sum(e, axis=-1, keepdims=True)
    aff_ref[...] = aff
    affT_ref[...] = aff.T[:N_EXPERTS]


def _post_attn(o_lat, o_ctx, xs, modsel, g, woT, wr):
    b, lt, d = xs.shape
    tt = TOKEN_TILE
    nt = lt // tt
    n_lat_tiles = nt - 1
    out_shape = (
        jax.ShapeDtypeStruct((b, lt, d), F32),
        jax.ShapeDtypeStruct((b, lt, d), BF16),
        jax.ShapeDtypeStruct((b, lt, LANES), F32),
        jax.ShapeDtypeStruct((b, N_EXPERTS, lt), F32),
    )
    return pl.pallas_call(
        _post_attn_kernel,
        out_shape=out_shape,
        grid=(b, nt),
        in_specs=[
            pl.BlockSpec((None, d, tt), lambda i, t: (i, 0, jnp.minimum(t, n_lat_tiles - 1))),
            pl.BlockSpec((None, d, tt), lambda i, t: (i, 0, 0)),
            pl.BlockSpec((None, tt, d), lambda i, t: (i, t, 0)),
            pl.BlockSpec((None, None, N_MOD, d), lambda i, t: (i, jnp.where(t == nt - 1, 0, 1), 0, 0)),
            pl.BlockSpec((1, d), lambda i, t: (0, 0)),
            pl.BlockSpec((d, d), lambda i, t: (0, 0)),
            pl.BlockSpec((2, d, LANES), lambda i, t: (0, 0, 0)),
        ],
        out_specs=(
            pl.BlockSpec((None, tt, d), lambda i, t: (i, t, 0)),
            pl.BlockSpec((None, tt, d), lambda i, t: (i, t, 0)),
            pl.BlockSpec((None, tt, LANES), lambda i, t: (i, t, 0)),
            pl.BlockSpec((None, N_EXPERTS, tt), lambda i, t: (i, 0, t)),
        ),
        compiler_params=_cparams(("parallel", "arbitrary"), VMEM_LIMIT),
        name="post_attn",
    )(o_lat, o_ctx, xs, modsel, g, woT, wr)


def _route_kernel(affT_ref, pos_ref, cnt_ref, *, n_lat, cap_lat, cap_ctx):
    tt = TOKEN_TILE
    lt = affT_ref.shape[1]
    ne = affT_ref.shape[0]
    r = lax.broadcasted_iota(jnp.int32, (tt, tt), 0)
    c = lax.broadcasted_iota(jnp.int32, (tt, tt), 1)
    strict_upper = jnp.where(r < c, 1.0, 0.0).astype(BF16)
    lane = lax.broadcasted_iota(jnp.int32, (ne, LANES), 1)
    cnt = jnp.zeros((ne, LANES), F32)

    def count(mask):
        return jnp.sum(jnp.where(mask, 1.0, 0.0), axis=1, keepdims=True)

    base = 0.0
    for lo, hi, cap in ((0, n_lat, cap_lat), (n_lat, lt, cap_ctx)):
        bits = pltpu.bitcast(affT_ref[:, lo:hi], jnp.int32)

        def bisect(step, thr, bits=bits, cap=cap):
            cand = thr | jnp.left_shift(jnp.int32(1), 30 - step)
            return jnp.where(count(bits >= cand) >= cap, cand, thr)

        thr = lax.fori_loop(0, 31, bisect, jnp.zeros((ne, 1), jnp.int32))
        need = cap - count(bits > thr)
        ties_before = jnp.zeros((ne, 1), F32)
        rows_before = jnp.zeros((ne, 1), F32) + base
        for ti in range((hi - lo) // tt):
            tile = lo // tt + ti
            b_t = bits[:, ti * tt:(ti + 1) * tt]
            eq = b_t == thr
            eq_f = jnp.where(eq, 1.0, 0.0)
            rank = jnp.dot(eq_f.astype(BF16), strict_upper, preferred_element_type=F32) + ties_before
            sel = (b_t > thr) | (eq & (rank < need))
            sel_f = jnp.where(sel, 1.0, 0.0)
            pos = jnp.dot(sel_f.astype(BF16), strict_upper, preferred_element_type=F32) + rows_before
            pos_ref[:, tile * tt:(tile + 1) * tt] = jnp.where(sel, pos, -1.0).astype(jnp.int32)
            cnt = jnp.where(lane == tile, rows_before, cnt)
            ties_before = ties_before + jnp.sum(eq_f, axis=1, keepdims=True)
            rows_before = rows_before + jnp.sum(sel_f, axis=1, keepdims=True)
        base = base + cap
    cnt = jnp.where(lane >= lt // tt, base, cnt)
    cnt_ref[...] = cnt.astype(jnp.int32)


def _route(affT, n_lat, cap_lat, cap_ctx):
    b, ne, lt = affT.shape
    return pl.pallas_call(
        functools.partial(_route_kernel, n_lat=n_lat, cap_lat=cap_lat, cap_ctx=cap_ctx),
        out_shape=(jax.ShapeDtypeStruct((b, ne, lt), jnp.int32),
                   jax.ShapeDtypeStruct((b, ne, LANES), jnp.int32)),
        grid=(b,),
        in_specs=[pl.BlockSpec((None, ne, lt), lambda i: (i, 0, 0))],
        out_specs=(pl.BlockSpec((None, ne, lt), lambda i: (i, 0, 0)),
                   pl.BlockSpec((None, ne, LANES), lambda i: (i, 0, 0))),
        compiler_params=_cparams(("arbitrary",), VMEM_LIMIT),
        name="route",
    )(affT)


def _gather_kernel(cnt_ref, pos_ref, h_ref, o_ref, *, cap):
    tt = TOKEN_TILE
    i, eg, s = pl.program_id(0), pl.program_id(1), pl.program_id(2)
    ng = o_ref.shape[0]
    sub = h_ref.shape[0] // tt
    last_start = cap - ROW_WINDOW

    @pl.when(s == 0)
    def _():
        o_ref[...] = jnp.zeros(o_ref.shape, BF16)

    for u in range(sub):
        r0, n_win = [], 0
        for k in range(ng):
            base = (i * N_EXPERTS + eg * ng + k) * LANES + s * sub + u
            c0, c1 = cnt_ref[base], cnt_ref[base + 1]
            r0.append((c0 // 16) * 16)
            n_win = jnp.maximum(n_win, jnp.where(c1 > c0, (c1 - r0[k] + ROW_WINDOW - 1) // ROW_WINDOW, 0))
        hb = h_ref[u * tt:(u + 1) * tt, :]

        def window(w, carry, r0=r0, hb=hb, u=u):
            starts, pieces = [], []
            for k in range(ng):
                want = r0[k] + w * ROW_WINDOW
                rs = pl.multiple_of(jnp.minimum(want, last_start), 16)
                rows = rs + lax.broadcasted_iota(jnp.int32, (ROW_WINDOW, tt), 0)
                hit = (pos_ref[k, :, u * tt:(u + 1) * tt] == rows) & (rows >= want)
                pieces.append(jnp.where(hit, 1.0, 0.0).astype(BF16))
                starts.append(rs)
            got = jnp.dot(jnp.concatenate(pieces, axis=0), hb, preferred_element_type=F32)
            for k in range(ng):
                o_ref[k, pl.ds(starts[k], ROW_WINDOW), :] += got[k * ROW_WINDOW:(k + 1) * ROW_WINDOW].astype(BF16)
            return carry

        lax.fori_loop(0, n_win, window, 0)


def _gather_tokens_per_step(nt):
    for k in (5, 4, 2, 1):
        if nt % k == 0:
            return k * TOKEN_TILE


def _gather(cnt_flat, pos4, h2, cap):
    b, lt, d = h2.shape
    gt = _gather_tokens_per_step(lt // TOKEN_TILE)
    ng = GATHER_EXPERT_GROUP
    grid_spec = pltpu.PrefetchScalarGridSpec(
        num_scalar_prefetch=1,
        grid=(b, N_EXPERTS // ng, lt // gt),
        in_specs=[pl.BlockSpec((None, ng, 1, gt), lambda i, e, s, cnt: (i, e, 0, s)),
                  pl.BlockSpec((None, gt, d), lambda i, e, s, cnt: (i, s, 0))],
        out_specs=pl.BlockSpec((None, ng, cap, d), lambda i, e, s, cnt: (i, e, 0, 0)),
    )
    return pl.pallas_call(
        functools.partial(_gather_kernel, cap=cap),
        out_shape=jax.ShapeDtypeStruct((b, N_EXPERTS, cap, d), BF16),
        grid_spec=grid_spec,
        compiler_params=_cparams(("parallel", "parallel", "arbitrary"), VMEM_LIMIT),
        name="moe_gather",
    )(cnt_flat, pos4, h2)


def _ffn_kernel(x_ref, wg_ref, wu_ref, wd_ref, y_ref):
    x = x_ref[...]
    acc = jnp.zeros((x.shape[0], wd_ref.shape[1]), F32)
    for f in range(wg_ref.shape[1] // FFN_SLICE):
        sl = slice(f * FFN_SLICE, (f + 1) * FFN_SLICE)
        g = jnp.dot(x, wg_ref[:, sl], preferred_element_type=F32)
        u = jnp.dot(x, wu_ref[:, sl], preferred_element_type=F32)
        a = (g * (1.0 / (1.0 + jnp.exp(-g))) * u).astype(BF16)
        acc = acc + jnp.dot(a, wd_ref[sl, :], preferred_element_type=F32)
    y_ref[...] = acc.astype(BF16)


def _ffn(x_sel, wg, wu, wd, layer):
    b, ne, cap, d = x_sel.shape
    f = wg.shape[3]
    n_row_tiles = 2 if cap % 32 == 0 else 1
    rt = cap // n_row_tiles
    return pl.pallas_call(
        _ffn_kernel,
        out_shape=jax.ShapeDtypeStruct((b, ne, cap, d), BF16),
        grid=(ne, b * n_row_tiles),
        in_specs=[
            pl.BlockSpec((None, None, rt, d), lambda e, r: (r // n_row_tiles, e, r % n_row_tiles, 0)),
            pl.BlockSpec((None, None, d, f), lambda e, r: (layer, e, 0, 0)),
            pl.BlockSpec((None, None, d, f), lambda e, r: (layer, e, 0, 0)),
            pl.BlockSpec((None, None, f, d), lambda e, r: (layer, e, 0, 0)),
        ],
        out_specs=pl.BlockSpec((None, None, rt, d), lambda e, r: (r // n_row_tiles, e, r % n_row_tiles, 0)),
        compiler_params=_cparams(("parallel", "arbitrary"), VMEM_LIMIT),
        name="moe_ffn",
    )(x_sel, wg, wu, wd)


def _combine_kernel(cnt_ref, x_ref, pos_ref, aff_ref, mod_ref, *rest, cap, final):
    if final:
        gfin_ref, y_hbm, o_ref, buf_ref, xbuf_ref, acc_ref, sem_ref, xsem_ref = rest
    else:
        y_hbm, o_ref, buf_ref, xbuf_ref, acc_ref, sem_ref, xsem_ref = rest
    tt = TOKEN_TILE
    i, t = pl.program_id(0), pl.program_id(1)
    n_t = pl.num_programs(1)
    ne = pos_ref.shape[0]
    last_start = cap - ROW_WINDOW
    step = i * n_t + t
    slot = step % 2

    def first_window(bi, ti, e):
        c0 = cnt_ref[(bi * ne + e) * LANES + ti]
        return pl.multiple_of(jnp.minimum((c0 // 16) * 16, last_start), 16)

    def window_copy(bi, ti, e, sl):
        rs = first_window(bi, ti, e)
        return pltpu.make_async_copy(y_hbm.at[bi, e, pl.ds(rs, ROW_WINDOW), :], buf_ref.at[sl, e], sem_ref.at[sl, e])

    @pl.when(step == 0)
    def _():
        for e in range(ne):
            window_copy(i, t, e, slot).start()

    @pl.when(step + 1 < pl.num_programs(0) * n_t)
    def _():
        wrap = t + 1 == n_t
        for e in range(ne):
            window_copy(jnp.where(wrap, i + 1, i), jnp.where(wrap, 0, t + 1), e, 1 - slot).start()

    pos_f = pos_ref[...].astype(F32)
    pad = jnp.full((LANES - ne, tt), -1.0, F32)
    posT = jnp.concatenate([pos_f, pad], axis=0).T
    lane = lax.broadcasted_iota(jnp.int32, (1, ROW_WINDOW), 1).astype(F32)

    gated = []
    for e in range(ne):
        hit = posT[:, e:e + 1] == first_window(i, t, e).astype(F32) + lane
        gated.append(jnp.where(hit, aff_ref[:, e:e + 1], 0.0).astype(BF16))
    for e in range(ne):
        window_copy(i, t, e, slot).wait()
    rows = buf_ref[slot].reshape(ne * ROW_WINDOW, buf_ref.shape[-1])
    acc_ref[...] = jnp.dot(jnp.concatenate(gated, axis=1), rows, preferred_element_type=F32)

    for e in range(ne):
        rs = first_window(i, t, e)
        c1 = cnt_ref[(i * ne + e) * LANES + t + 1]
        pcol = posT[:, e:e + 1]
        gate = aff_ref[:, e:e + 1]
        n_extra = jnp.maximum(c1 - rs - 1, 0) // ROW_WINDOW

        def extra(w, carry, e=e, rs=rs, pcol=pcol, gate=gate):
            want = rs + (w + 1) * ROW_WINDOW
            start = pl.multiple_of(jnp.minimum(want, last_start), 16)
            cp = pltpu.make_async_copy(y_hbm.at[i, e, pl.ds(start, ROW_WINDOW), :], xbuf_ref, xsem_ref)
            cp.start()
            cp.wait()
            row = start.astype(F32) + lane
            hit = (pcol == row) & (row >= want.astype(F32))
            acc_ref[...] += gate * jnp.dot(jnp.where(hit, 1.0, 0.0).astype(BF16), xbuf_ref[...],
                                           preferred_element_type=F32)
            return carry

        lax.fori_loop(0, n_extra, extra, 0)

    x2 = x_ref[...] + mod_ref[5:6, :] * acc_ref[...]
    if final:
        ms = jnp.mean(x2 * x2, axis=-1, keepdims=True)
        x2 = x2 * lax.rsqrt(ms + NORM_EPS) * gfin_ref[...]
    o_ref[...] = x2


def _combine(cnt_flat, x1, pos, aff, modsel, y, n_lat, final_g=None):
    b, lt, d = x1.shape
    tt = TOKEN_TILE
    nt = lt // tt
    cap = y.shape[2]
    final = final_g is not None
    n_steps = n_lat // tt if final else nt
    n_out = n_lat if final else lt
    in_specs = [
        pl.BlockSpec((None, tt, d), lambda i, t, cnt: (i, t, 0)),
        pl.BlockSpec((None, N_EXPERTS, tt), lambda i, t, cnt: (i, 0, t)),
        pl.BlockSpec((None, tt, LANES), lambda i, t, cnt: (i, t, 0)),
        pl.BlockSpec((None, None, N_MOD, d), lambda i, t, cnt: (i, jnp.where(t == nt - 1, 0, 1), 0, 0)),
    ]
    args = [x1, pos, aff, modsel]
    if final:
        in_specs.append(pl.BlockSpec((1, d), lambda i, t, cnt: (0, 0)))
        args.append(final_g)
    in_specs.append(pl.BlockSpec(memory_space=pl.ANY))
    args.append(y)
    grid_spec = pltpu.PrefetchScalarGridSpec(
        num_scalar_prefetch=1,
        grid=(b, n_steps),
        in_specs=in_specs,
        out_specs=pl.BlockSpec((None, tt, d), lambda i, t, cnt: (i, t, 0)),
        scratch_shapes=[
            pltpu.VMEM((2, N_EXPERTS, ROW_WINDOW, d), BF16),
            pltpu.VMEM((ROW_WINDOW, d), BF16),
            pltpu.VMEM((tt, d), F32),
            pltpu.SemaphoreType.DMA((2, N_EXPERTS)),
            pltpu.SemaphoreType.DMA(()),
        ],
    )
    return pl.pallas_call(
        functools.partial(_combine_kernel, cap=cap, final=final),
        out_shape=jax.ShapeDtypeStruct((b, n_out, d), F32),
        grid_spec=grid_spec,
        compiler_params=_cparams(("arbitrary", "arbitrary"), VMEM_LIMIT),
        name="moe_combine",
    )(cnt_flat, *args)


def _rope_tables(n_lat, n_ctx):
    rows = n_lat // GRID_W
    row = jnp.repeat(jnp.arange(rows, dtype=F32), GRID_W)
    col = jnp.tile(jnp.arange(GRID_W, dtype=F32), rows)
    inv_freq = 1.0 / (ROPE_THETA ** (jnp.arange(0, ROPE_AXIS_DIM, 2, dtype=F32) / ROPE_AXIS_DIM))
    ang_r = (row[:, None] * inv_freq).T
    ang_c = (col[:, None] * inv_freq).T
    ct = jnp.concatenate([jnp.cos(ang_r), jnp.cos(ang_r), jnp.cos(ang_c), jnp.cos(ang_c)], axis=0)
    st = jnp.concatenate([-jnp.sin(ang_r), jnp.sin(ang_r), -jnp.sin(ang_c), jnp.sin(ang_c)], axis=0)
    ct = jnp.concatenate([ct, jnp.ones((HEAD_DIM, n_ctx), F32)], axis=1)
    st = jnp.concatenate([st, jnp.zeros((HEAD_DIM, n_ctx), F32)], axis=1)
    return ct, st


def kernel(x, c, ctx, c_ctx, w_mod, b_mod, norm_mix_g, norm_ffn_g, w_qkv, w_o, q_norm_g, k_norm_g,
           attn_sink, w_router, w_gate, w_up, w_down, final_norm_g):
    b, n_lat, d = x.shape
    n_ctx = ctx.shape[1]
    depth = w_mod.shape[0]
    tt = TOKEN_TILE
    assert n_ctx == tt and n_lat % tt == 0 and d == N_Q_HEADS * HEAD_DIM
    cap_lat = max(1, EC_CAPACITY * n_lat // N_EXPERTS)
    cap_ctx = max(1, EC_CAPACITY * n_ctx // N_EXPERTS)
    cap = cap_lat + cap_ctx
    assert cap % 16 == 0 and cap >= ROW_WINDOW and (cap - ROW_WINDOW) % 16 == 0

    cond = jnp.concatenate([c, c_ctx[None, :], jnp.zeros((8 - b - 1, d), F32)], axis=0)
    mod = _modulation(cond, w_mod, b_mod)
    mod = mod.reshape(depth, 8, N_MOD, d)
    modsel = jnp.stack([jnp.broadcast_to(mod[:, b][:, None], (depth, b, N_MOD, d)), mod[:, :b]], axis=2)

    ct, st = _rope_tables(n_lat, n_ctx)
    xs = jnp.concatenate([x, ctx], axis=1)
    wg, wu, wd = w_gate.astype(BF16), w_up.astype(BF16), w_down.astype(BF16)
    n_qk =(N_Q_HEADS + N_KV_HEADS) * HEAD_DIM

    for i in range(depth):
        last = i == depth - 1
        is_global = i % 2 == 0
        j = i // 2
        wT = w_qkv[i].T.astype(BF16)
        if is_global:
            gqk = jnp.concatenate([jnp.tile(q_norm_g[j], N_Q_HEADS), jnp.tile(k_norm_g[j], N_KV_HEADS)])
        else:
            gqk = jnp.ones((n_qk,), F32)
        gqk = jnp.broadcast_to(gqk[:, None], (n_qk, tt))
        qT, k, vT = _pre_attn(xs, modsel[i], norm_mix_g[i][None, :], wT, ct, st, gqk, is_global)
        if is_global:
            o_lat = _attn_global(qT, k, vT, n_lat)
            o_ctx = _attn_ctx(qT, k, vT, None, n_lat)
        else:
            sink = jnp.repeat(attn_sink[j].reshape(N_KV_HEADS, GQA_GROUP) * LOG2_E, tt, axis=1)[:, None, :]
            o_lat = _attn_window(qT, k, vT, sink, n_lat)
            o_ctx = _attn_ctx(qT, k, vT, sink, n_lat)
        wr = jnp.pad(w_router[i], ((0, 0), (0, LANES - N_EXPERTS)))
        wr_hi = wr.astype(BF16)
        wr = jnp.stack([wr_hi, (wr - wr_hi.astype(F32)).astype(BF16)])
        x1, h2, aff, affT = _post_attn(o_lat.reshape(b, d, n_lat), o_ctx.reshape(b, d, n_ctx), xs, modsel[i],
                                       norm_ffn_g[i][None, :], w_o[i].T.astype(BF16), wr)
        pos, cnt = _route(affT, n_lat, cap_lat, cap_ctx)
        cnt_flat = cnt.reshape(-1)
        x_sel = _gather(cnt_flat, pos.reshape(b, N_EXPERTS, 1, n_lat + n_ctx), h2, cap)
        y = _ffn(x_sel, wg, wu, wd, i)
        xs = _combine(cnt_flat, x1, pos, aff, modsel[i], y, n_lat,
                      final_g=final_norm_g[None, :] if last else None)
    return xs
```

```python
import functools

import jax
import jax.numpy as jnp
from jax import lax
from jax.experimental import pallas as pl
from jax.experimental.pallas import tpu as pltpu

HEAD_DIM = 64
N_KV_HEADS = 4
GQA_GROUP = 4
N_Q_HEADS = N_KV_HEADS * GQA_GROUP
ROPE_AXIS_DIM = HEAD_DIM // 2
ROPE_HALF = ROPE_AXIS_DIM // 2
ROPE_THETA = 10000.0
GRID_W = 64
WINDOW = 128
N_EXPERTS = 16
EC_CAPACITY = 2
N_MOD = 6
NORM_EPS = 1e-6
NEG_BIG = -1e30
LOG2_E = 1.4426950408889634
Q_SCALE = HEAD_DIM ** -0.5 * LOG2_E

TOKEN_TILE = 256
LANES = 128
V_ROWS = 80
ROW_WINDOW = 128
FFN_SLICE = 512
GATHER_EXPERT_GROUP = 4
SOFTMAX_LAG = 1
PV_LAG = 3
VMEM_LIMIT = 56 * 1024 * 1024

F32 = jnp.float32
BF16 = jnp.bfloat16


def _cparams(sem, vmem=None):
    return pltpu.CompilerParams(dimension_semantics=sem, vmem_limit_bytes=vmem)


def _mod_kernel(c_ref, w_ref, b_ref, o_ref):
    c = c_ref[...]
    s = c * (1.0 / (1.0 + jnp.exp(-c)))
    o_ref[...] = jnp.dot(s, w_ref[...], preferred_element_type=F32,
                         precision=lax.Precision.HIGHEST) + b_ref[...]


def _modulation(cond, w_mod, b_mod):
    depth, d, n = w_mod.shape
    nb = n // 4
    return pl.pallas_call(
        _mod_kernel,
        out_shape=jax.ShapeDtypeStruct((depth, 8, n), F32),
        grid=(depth, n // nb),
        in_specs=[pl.BlockSpec((8, d), lambda l, j: (0, 0)),
                  pl.BlockSpec((None, d, nb), lambda l, j: (l, 0, j)),
                  pl.BlockSpec((None, 1, nb), lambda l, j: (l, 0, j))],
        out_specs=pl.BlockSpec((None, 8, nb), lambda l, j: (l, 0, j)),
        compiler_params=_cparams(("arbitrary", "arbitrary"), VMEM_LIMIT),
        name="modulation",
    )(cond, w_mod, b_mod.reshape(depth, 1, n))


def _pre_attn_kernel(x_ref, mod_ref, g_ref, wT_ref, ct_ref, st_ref, gqk_ref,
                     q_ref, k_ref, v_ref, *, qk_norm):
    tt = x_ref.shape[0]
    x = x_ref[...]
    ms = jnp.mean(x * x, axis=-1, keepdims=True)
    h = x * lax.rsqrt(ms + NORM_EPS) * g_ref[...]
    h = h * (1.0 + mod_ref[1:2, :]) + mod_ref[0:1, :]
    qkv = lax.dot_general(wT_ref[...], h.astype(BF16), (((1,), (1,)), ((), ())),
                          preferred_element_type=F32)
    n_qk = (N_Q_HEADS + N_KV_HEADS) * HEAD_DIM
    n_heads = N_Q_HEADS + N_KV_HEADS
    qk = qkv[:n_qk].reshape(n_heads, HEAD_DIM, tt)
    if qk_norm:
        ss = jnp.mean(qk * qk, axis=1, keepdims=True)
        qk = qk * lax.rsqrt(ss + NORM_EPS) * gqk_ref[...].reshape(n_heads, HEAD_DIM, tt)
    q5 = qk.reshape(n_heads, 2, 2, ROPE_HALF, tt)
    swapped = jnp.concatenate([q5[:, :, 1:2], q5[:, :, 0:1]], axis=2).reshape(n_heads, HEAD_DIM, tt)
    qk = qk * ct_ref[...][None] + swapped * st_ref[...][None]

    zeros64 = jnp.zeros((HEAD_DIM, tt), BF16)
    for hh in range(N_KV_HEADS):
        for g in range(GQA_GROUP):
            piece = (qk[hh * GQA_GROUP + g] * Q_SCALE).astype(BF16)
            pair = [piece, zeros64] if hh % 2 == 0 else [zeros64, piece]
            q_ref[hh, :, g * tt:(g + 1) * tt] = jnp.concatenate(pair, axis=0)
    kT = qk[N_Q_HEADS:].reshape(N_KV_HEADS * HEAD_DIM, tt)
    for p in range(N_KV_HEADS // 2):
        k_ref[p] = kT[p * LANES:(p + 1) * LANES].T.astype(BF16)
    v = qkv[n_qk:]
    row = lax.broadcasted_iota(jnp.int32, (V_ROWS - HEAD_DIM, tt), 0)
    tail = jnp.where(row == 0, 1.0, 0.0).astype(BF16)
    for hh in range(N_KV_HEADS):
        v_ref[hh] = jnp.concatenate([v[hh * HEAD_DIM:(hh + 1) * HEAD_DIM].astype(BF16), tail], axis=0)


def _pre_attn(xs, modsel, g, wT, ct, st, gqk, qk_norm):
    b, lt, d = xs.shape
    tt = TOKEN_TILE
    nt = lt // tt
    n_qkv = wT.shape[0]
    n_qk = gqk.shape[0]
    out_shape = (
        jax.ShapeDtypeStruct((b, N_KV_HEADS, nt, 2 * HEAD_DIM, GQA_GROUP * tt), BF16),
        jax.ShapeDtypeStruct((b, N_KV_HEADS // 2, lt, 2 * HEAD_DIM), BF16),
        jax.ShapeDtypeStruct((b, N_KV_HEADS, V_ROWS, lt), BF16),
    )
    return pl.pallas_call(
        functools.partial(_pre_attn_kernel, qk_norm=qk_norm),
        out_shape=out_shape,
        grid=(b, nt),
        in_specs=[
            pl.BlockSpec((None, tt, d), lambda i, t: (i, t, 0)),
            pl.BlockSpec((None, None, N_MOD, d), lambda i, t: (i, jnp.where(t == nt - 1, 0, 1), 0, 0)),
            pl.BlockSpec((1, d), lambda i, t: (0, 0)),
            pl.BlockSpec((n_qkv, d), lambda i, t: (0, 0)),
            pl.BlockSpec((HEAD_DIM, tt), lambda i, t: (0, t)),
            pl.BlockSpec((HEAD_DIM, tt), lambda i, t: (0, t)),
            pl.BlockSpec((n_qk, tt), lambda i, t: (0, 0)),
        ],
        out_specs=(
            pl.BlockSpec((None, N_KV_HEADS, None, 2 * HEAD_DIM, GQA_GROUP * tt), lambda i, t: (i, 0, t, 0, 0)),
            pl.BlockSpec((None, N_KV_HEADS // 2, tt, 2 * HEAD_DIM), lambda i, t: (i, 0, t, 0)),
            pl.BlockSpec((None, N_KV_HEADS, V_ROWS, tt), lambda i, t: (i, 0, 0, t)),
        ),
        compiler_params=_cparams(("parallel", "arbitrary"), VMEM_LIMIT),
        name="pre_attn",
    )(xs, modsel, g, wT, ct, st, gqk)


def _attn_global_kernel(q_ref, k_ref, v_ref, kc_ref, vc_ref, *rest, chunk, n_cast):
    o_ref, m_ref, acc_ref = rest[n_cast], rest[-2], rest[-1]
    for src, dst in zip(rest[:n_cast], rest[n_cast + 1:2 * n_cast + 1]):
        dst[...] = src[...].astype(BF16)
    j = pl.program_id(3)
    n_tiles = q_ref.shape[0]
    tt = q_ref.shape[2] // GQA_GROUP
    tk = k_ref.shape[0]
    blocks = [(a, g) for a in range(n_tiles) for g in range(GQA_GROUP)]

    def cols(n):
        return slice(n * tt, (n + 1) * tt)

    def absorb(kr, vr, n_keys, step, m, acc):
        work = [(c, n) for c in range(n_keys // step) for n in range(len(blocks))]

        def scores(c, n):
            a, g = blocks[n]
            return jnp.dot(kr[c * step:(c + 1) * step, :], q_ref[a, :, cols(g)], preferred_element_type=F32)

        s_of, p_of, alpha_of = {}, {}, {}
        for t in range(len(work) + PV_LAG):
            if t < len(work):
                s_of[t] = scores(*work[t])
            if 0 <= t - SOFTMAX_LAG < len(work):
                u = t - SOFTMAX_LAG
                n = work[u][1]
                s = s_of.pop(u)
                m_new = jnp.maximum(m[n], jnp.max(s, axis=0, keepdims=True))
                p_of[u] = jnp.exp2(s - m_new).astype(BF16)
                alpha_of[u] = jnp.exp2(m[n] - m_new)
                m[n] = m_new
            if 0 <= t - PV_LAG < len(work):
                u = t - PV_LAG
                c, n = work[u]
                acc[n] = acc[n] * alpha_of.pop(u) + jnp.dot(vr[:, c * step:(c + 1) * step], p_of.pop(u),
                                                           preferred_element_type=F32)

    def save(m, acc):
        for n in range(len(blocks)):
            m_ref[:, cols(n)] = m[n]
            acc_ref[:, cols(n)] = acc[n]

    @pl.when(j == 0)
    def _():
        m0 = [jnp.full((1, tt), NEG_BIG, F32) for _ in blocks]
        acc0 = [jnp.zeros((V_ROWS, tt), F32) for _ in blocks]
        absorb(kc_ref, vc_ref, kc_ref.shape[0], kc_ref.shape[0], m0, acc0)
        save(m0, acc0)

    m = [m_ref[:, cols(n)] for n in range(len(blocks))]
    acc = [acc_ref[:, cols(n)] for n in range(len(blocks))]
    absorb(k_ref, v_ref, tk, chunk, m, acc)
    save(m, acc)

    @pl.when(j == pl.num_programs(3) - 1)
    def _():
        for n, (a, g) in enumerate(blocks):
            o = acc[n][:HEAD_DIM] / acc[n][HEAD_DIM:HEAD_DIM + 1]
            o_ref[g, :, a * tt:(a + 1) * tt] = o.astype(BF16)


def _attn_global(qT, k, vT, n_lat, to_cast):
    b = qT.shape[0]
    tt = TOKEN_TILE
    n_lat_tiles = n_lat // tt
    q_tiles = 2 if n_lat_tiles % 2 == 0 else 1
    tk = 8192 if n_lat % 8192 == 0 else tt
    chunk = 512 if tk % 512 == 0 else tt
    tq = q_tiles * tt
    ctx_tile = n_lat // tt
    n_cols = q_tiles * GQA_GROUP * tt
    n_q, n_k = n_lat // tq, n_lat // tk
    n_steps = b * N_KV_HEADS * n_q * n_k

    def slab(first):
        return lambda i, h, qi, kj: (first + ((i * N_KV_HEADS + h) * n_q + qi) * n_k + kj, 0)

    src_specs, dst_specs, dst_shapes = [], [], []
    for w, first_row, n_rows in to_cast:
        rows = n_rows // n_steps
        assert n_rows % (16 * n_steps) == 0 and first_row % rows == 0
        src_specs.append(pl.BlockSpec((rows, w.shape[1]), slab(first_row // rows)))
        dst_specs.append(pl.BlockSpec((rows, w.shape[1]), slab(0)))
        dst_shapes.append(jax.ShapeDtypeStruct((n_rows, w.shape[1]), BF16))
    out = pl.pallas_call(
        functools.partial(_attn_global_kernel, chunk=chunk, n_cast=len(to_cast)),
        out_shape=[jax.ShapeDtypeStruct((b, N_Q_HEADS, HEAD_DIM, n_lat), BF16)] + dst_shapes,
        grid=(b, N_KV_HEADS, n_q, n_k),
        in_specs=[
            pl.BlockSpec((None, None, q_tiles, 2 * HEAD_DIM, GQA_GROUP * tt), lambda i, h, qi, kj: (i, h, qi, 0, 0)),
            pl.BlockSpec((None, None, tk, 2 * HEAD_DIM), lambda i, h, qi, kj: (i, h // 2, kj, 0)),
            pl.BlockSpec((None, None, V_ROWS, tk), lambda i, h, qi, kj: (i, h, 0, kj)),
            pl.BlockSpec((None, None, tt, 2 * HEAD_DIM), lambda i, h, qi, kj: (i, h // 2, ctx_tile, 0)),
            pl.BlockSpec((None, None, V_ROWS, tt), lambda i, h, qi, kj: (i, h, 0, ctx_tile)),
        ] + src_specs,
        out_specs=[pl.BlockSpec((None, GQA_GROUP, HEAD_DIM, tq), lambda i, h, qi, kj: (i, h, 0, qi))] + dst_specs,
        scratch_shapes=[pltpu.VMEM((1, n_cols), F32), pltpu.VMEM((V_ROWS, n_cols), F32)],
        compiler_params=_cparams(("arbitrary", "arbitrary", "arbitrary", "arbitrary"), VMEM_LIMIT),
        name="attn_global",
    )(qT, k, vT, k, vT, *[w for w, _, _ in to_cast])
    return out[0], out[1:]


def _attn_ctx_kernel(*refs, has_sink):
    if has_sink:
        q_ref, kc_ref, vc_ref, sink_ref, o_ref = refs
    else:
        q_ref, kc_ref, vc_ref, o_ref = refs
    tt = q_ref.shape[1] // GQA_GROUP
    s = jnp.dot(kc_ref[...], q_ref[...], preferred_element_type=F32)
    m = jnp.max(s, axis=0, keepdims=True)
    if has_sink:
        sink = sink_ref[...]
        m = jnp.maximum(m, sink)
    p = jnp.exp2(s - m).astype(BF16)
    acc = jnp.dot(vc_ref[...], p, preferred_element_type=F32)
    denom = acc[HEAD_DIM:HEAD_DIM + 1]
    if has_sink:
        denom = denom + jnp.exp2(sink - m)
    o = acc[:HEAD_DIM] / denom
    for g in range(GQA_GROUP):
        o_ref[g] = o[:, g * tt:(g + 1) * tt].astype(BF16)


def _attn_ctx(qT, k, vT, sink, n_lat):
    b = qT.shape[0]
    tt = TOKEN_TILE
    ctx_tile = n_lat // tt
    has_sink = sink is not None
    in_specs = [
        pl.BlockSpec((None, None, None, 2 * HEAD_DIM, GQA_GROUP * tt), lambda i, h: (i, h, ctx_tile, 0, 0)),
        pl.BlockSpec((None, None, tt, 2 * HEAD_DIM), lambda i, h: (i, h // 2, ctx_tile, 0)),
        pl.BlockSpec((None, None, V_ROWS, tt), lambda i, h: (i, h, 0, ctx_tile)),
    ]
    args = [qT, k, vT]
    if has_sink:
        in_specs.append(pl.BlockSpec((None, 1, GQA_GROUP * tt), lambda i, h: (h, 0, 0)))
        args.append(sink)
    return pl.pallas_call(
        functools.partial(_attn_ctx_kernel, has_sink=has_sink),
        out_shape=jax.ShapeDtypeStruct((b, N_Q_HEADS, HEAD_DIM, tt), BF16),
        grid=(b, N_KV_HEADS),
        in_specs=in_specs,
        out_specs=pl.BlockSpec((None, GQA_GROUP, HEAD_DIM, tt), lambda i, h: (i, h, 0, 0)),
        compiler_params=_cparams(("parallel", "parallel"), VMEM_LIMIT),
        name="attn_ctx",
    )(*args)


def _attn_window_kernel(q_ref, kc_ref, vc_ref, kp_ref, k_ref, kn_ref, vp_ref, v_ref, vn_ref, sink_ref, o_ref):
    i = pl.program_id(2)
    n_tiles = q_ref.shape[0]
    tt = q_ref.shape[2] // GQA_GROUP
    span = tt + 2 * WINDOW
    k_win = jnp.concatenate([kp_ref[...], k_ref[...], kn_ref[...]], axis=0)
    v_win = jnp.concatenate([vp_ref[...], v_ref[...], vn_ref[...]], axis=1)
    kk = lax.broadcasted_iota(jnp.int32, (span, tt), 0)
    t = lax.broadcasted_iota(jnp.int32, (span, tt), 1)
    in_band = (kk >= t) & (kk <= t + 2 * WINDOW)
    top = jnp.zeros((kc_ref.shape[0], tt), F32)
    keys, vals, bias = [], [], []
    for a in range(n_tiles):
        lo = jnp.where((i == 0) & (a == 0), WINDOW, 0)
        hi = jnp.where((i == pl.num_programs(2) - 1) & (a == n_tiles - 1), WINDOW + tt, span)
        ok = in_band & (kk >= lo) & (kk < hi)
        bias.append(jnp.concatenate([top, jnp.where(ok, 0.0, NEG_BIG)], axis=0))
        keys.append(jnp.concatenate([kc_ref[...], k_win[a * tt:a * tt + span]], axis=0))
        vals.append(jnp.concatenate([vc_ref[...], v_win[:, a * tt:a * tt + span]], axis=1))
    work = [(a, g) for a in range(n_tiles) for g in range(GQA_GROUP)]

    s_of, p_of, d_of = {}, {}, {}
    for step in range(len(work) + PV_LAG):
        if step < len(work):
            a, g = work[step]
            s_of[step] = jnp.dot(keys[a], q_ref[a, :, g * tt:(g + 1) * tt], preferred_element_type=F32) + bias[a]
        if 0 <= step - SOFTMAX_LAG < len(work):
            u = step - SOFTMAX_LAG
            g = work[u][1]
            sink = sink_ref[:, g * tt:(g + 1) * tt]
            s = s_of.pop(u)
            m = jnp.maximum(jnp.max(s, axis=0, keepdims=True), sink)
            p_of[u] = jnp.exp2(s - m).astype(BF16)
            d_of[u] = jnp.exp2(sink - m)
        if 0 <= step - PV_LAG < len(work):
            u = step - PV_LAG
            a, g = work[u]
            acc = jnp.dot(vals[a], p_of.pop(u), preferred_element_type=F32)
            o = acc[:HEAD_DIM] / (acc[HEAD_DIM:HEAD_DIM + 1] + d_of.pop(u))
            o_ref[g, :, a * tt:(a + 1) * tt] = o.astype(BF16)


def _attn_window(qT, k, vT, sink, n_lat):
    b = qT.shape[0]
    tt = TOKEN_TILE
    ctx_tile = n_lat // tt
    n_lat_tiles = n_lat // tt
    q_tiles = next(n for n in (4, 2, 1) if n_lat_tiles % n == 0)
    tq = q_tiles * tt
    per_step = tq // WINDOW
    last_blk = n_lat // WINDOW - 1
    prev = lambda t: jnp.maximum(t * per_step - 1, 0)
    nxt = lambda t: jnp.minimum((t + 1) * per_step, last_blk)
    in_specs = [
        pl.BlockSpec((None, None, q_tiles, 2 * HEAD_DIM, GQA_GROUP * tt), lambda i, h, t: (i, h, t, 0, 0)),
        pl.BlockSpec((None, None, tt, 2 * HEAD_DIM), lambda i, h, t: (i, h // 2, ctx_tile, 0)),
        pl.BlockSpec((None, None, V_ROWS, tt), lambda i, h, t: (i, h, 0, ctx_tile)),
        pl.BlockSpec((None, None, WINDOW, 2 * HEAD_DIM), lambda i, h, t: (i, h // 2, prev(t), 0)),
        pl.BlockSpec((None, None, tq, 2 * HEAD_DIM), lambda i, h, t: (i, h // 2, t, 0)),
        pl.BlockSpec((None, None, WINDOW, 2 * HEAD_DIM), lambda i, h, t: (i, h // 2, nxt(t), 0)),
        pl.BlockSpec((None, None, V_ROWS, WINDOW), lambda i, h, t: (i, h, 0, prev(t))),
        pl.BlockSpec((None, None, V_ROWS, tq), lambda i, h, t: (i, h, 0, t)),
        pl.BlockSpec((None, None, V_ROWS, WINDOW), lambda i, h, t: (i, h, 0, nxt(t))),
        pl.BlockSpec((None, 1, GQA_GROUP * tt), lambda i, h, t: (h, 0, 0)),
    ]
    return pl.pallas_call(
        _attn_window_kernel,
        out_shape=jax.ShapeDtypeStruct((b, N_Q_HEADS, HEAD_DIM, n_lat), BF16),
        grid=(b, N_KV_HEADS, n_lat // tq),
        in_specs=in_specs,
        out_specs=pl.BlockSpec((None, GQA_GROUP, HEAD_DIM, tq), lambda i, h, t: (i, h, 0, t)),
        compiler_params=_cparams(("parallel", "parallel", "arbitrary"), VMEM_LIMIT),
        name="attn_window",
    )(qT, k, vT, k, k, k, vT, vT, vT, sink)


def _post_attn_kernel(ol_ref, oc_ref, x_ref, mod_ref, g_ref, woT_ref, wr_ref,
                      x1_ref, h2_ref, aff_ref, affT_ref):
    t = pl.program_id(1)
    is_ctx = t == pl.num_programs(1) - 1
    oT = jnp.where(is_ctx, oc_ref[...], ol_ref[...])
    yT = jnp.dot(woT_ref[...], oT, preferred_element_type=F32)
    x1 = x_ref[...] + mod_ref[2:3, :] * yT.T
    x1_ref[...] = x1
    ms = jnp.mean(x1 * x1, axis=-1, keepdims=True)
    h2 = x1 * lax.rsqrt(ms + NORM_EPS) * g_ref[...]
    h2 = h2 * (1.0 + mod_ref[4:5, :]) + mod_ref[3:4, :]
    h2_ref[...] = h2.astype(BF16)
    h_hi = h2.astype(BF16)
    h_lo = (h2 - h_hi.astype(F32)).astype(BF16)
    logits = (jnp.dot(h_hi, wr_ref[0], preferred_element_type=F32)
              + jnp.dot(h_lo, wr_ref[0], preferred_element_type=F32)
              + jnp.dot(h_hi, wr_ref[1], preferred_element_type=F32))
    lane = lax.broadcasted_iota(jnp.int32, logits.shape, 1)
    logits = jnp.where(lane < N_EXPERTS, logits, NEG_BIG)
    e = jnp.exp(logits - jnp.max(logits, axis=-1, keepdims=True))
    aff = e / jnp.sum(e, axis=-1, keepdims=True)
    aff_ref[...] = aff
    affT_ref[...] = aff.T[:N_EXPERTS]


def _post_attn(o_lat, o_ctx, xs, modsel, g, woT, wr):
    b, lt, d = xs.shape
    tt = TOKEN_TILE
    nt = lt // tt
    n_lat_tiles = nt - 1
    out_shape = (
        jax.ShapeDtypeStruct((b, lt, d), F32),
        jax.ShapeDtypeStruct((b, lt, d), BF16),
        jax.ShapeDtypeStruct((b, lt, LANES), F32),
        jax.ShapeDtypeStruct((b, N_EXPERTS, lt), F32),
    )
    return pl.pallas_call(
        _post_attn_kernel,
        out_shape=out_shape,
        grid=(b, nt),
        in_specs=[
            pl.BlockSpec((None, d, tt), lambda i, t: (i, 0, jnp.minimum(t, n_lat_tiles - 1))),
            pl.BlockSpec((None, d, tt), lambda i, t: (i, 0, 0)),
            pl.BlockSpec((None, tt, d), lambda i, t: (i, t, 0)),
            pl.BlockSpec((None, None, N_MOD, d), lambda i, t: (i, jnp.where(t == nt - 1, 0, 1), 0, 0)),
            pl.BlockSpec((1, d), lambda i, t: (0, 0)),
            pl.BlockSpec((d, d), lambda i, t: (0, 0)),
            pl.BlockSpec((2, d, LANES), lambda i, t: (0, 0, 0)),
        ],
        out_specs=(
            pl.BlockSpec((None, tt, d), lambda i, t: (i, t, 0)),
            pl.BlockSpec((None, tt, d), lambda i, t: (i, t, 0)),
            pl.BlockSpec((None, tt, LANES), lambda i, t: (i, t, 0)),
            pl.BlockSpec((None, N_EXPERTS, tt), lambda i, t: (i, 0, t)),
        ),
        compiler_params=_cparams(("parallel", "arbitrary"), VMEM_LIMIT),
        name="post_attn",
    )(o_lat, o_ctx, xs, modsel, g, woT, wr)


def _route_kernel(affT_ref, pos_ref, cnt_ref, *, n_lat, cap_lat, cap_ctx):
    tt = TOKEN_TILE
    lt = affT_ref.shape[1]
    ne = affT_ref.shape[0]
    r = lax.broadcasted_iota(jnp.int32, (tt, tt), 0)
    c = lax.broadcasted_iota(jnp.int32, (tt, tt), 1)
    strict_upper = jnp.where(r < c, 1.0, 0.0).astype(BF16)
    lane = lax.broadcasted_iota(jnp.int32, (ne, LANES), 1)
    cnt = jnp.zeros((ne, LANES), F32)

    def count(mask):
        return jnp.sum(jnp.where(mask, 1.0, 0.0), axis=1, keepdims=True)

    base = 0.0
    for lo, hi, cap in ((0, n_lat, cap_lat), (n_lat, lt, cap_ctx)):
        bits = pltpu.bitcast(affT_ref[:, lo:hi], jnp.int32)

        def bisect(step, thr, bits=bits, cap=cap):
            cand = thr | jnp.left_shift(jnp.int32(1), 30 - step)
            return jnp.where(count(bits >= cand) >= cap, cand, thr)

        thr = lax.fori_loop(0, 31, bisect, jnp.zeros((ne, 1), jnp.int32))
        need = cap - count(bits > thr)
        ties_before = jnp.zeros((ne, 1), F32)
        rows_before = jnp.zeros((ne, 1), F32) + base
        for ti in range((hi - lo) // tt):
            tile = lo // tt + ti
            b_t = bits[:, ti * tt:(ti + 1) * tt]
            eq = b_t == thr
            eq_f = jnp.where(eq, 1.0, 0.0)
            rank = jnp.dot(eq_f.astype(BF16), strict_upper, preferred_element_type=F32) + ties_before
            sel = (b_t > thr) | (eq & (rank < need))
            sel_f = jnp.where(sel, 1.0, 0.0)
            pos = jnp.dot(sel_f.astype(BF16), strict_upper, preferred_element_type=F32) + rows_before
            pos_ref[:, tile * tt:(tile + 1) * tt] = jnp.where(sel, pos, -1.0).astype(jnp.int32)
            cnt = jnp.where(lane == tile, rows_before, cnt)
            ties_before = ties_before + jnp.sum(eq_f, axis=1, keepdims=True)
            rows_before = rows_before + jnp.sum(sel_f, axis=1, keepdims=True)
        base = base + cap
    cnt = jnp.where(lane >= lt // tt, base, cnt)
    cnt_ref[...] = cnt.astype(jnp.int32)


def _route(affT, n_lat, cap_lat, cap_ctx):
    b, ne, lt = affT.shape
    return pl.pallas_call(
        functools.partial(_route_kernel, n_lat=n_lat, cap_lat=cap_lat, cap_ctx=cap_ctx),
        out_shape=(jax.ShapeDtypeStruct((b, ne, lt), jnp.int32),
                   jax.ShapeDtypeStruct((b, ne, LANES), jnp.int32)),
        grid=(b,),
        in_specs=[pl.BlockSpec((None, ne, lt), lambda i: (i, 0, 0))],
        out_specs=(pl.BlockSpec((None, ne, lt), lambda i: (i, 0, 0)),
                   pl.BlockSpec((None, ne, LANES), lambda i: (i, 0, 0))),
        compiler_params=_cparams(("arbitrary",), VMEM_LIMIT),
        name="route",
    )(affT)


def _gather_kernel(cnt_ref, pos_ref, h_ref, o_ref, *, cap):
    tt = TOKEN_TILE
    i, eg, s = pl.program_id(0), pl.program_id(1), pl.program_id(2)
    ng = o_ref.shape[0]
    sub = h_ref.shape[0] // tt
    last_start = cap - ROW_WINDOW

    @pl.when(s == 0)
    def _():
        o_ref[...] = jnp.zeros(o_ref.shape, BF16)

    for u in range(sub):
        r0, n_win = [], 0
        for k in range(ng):
            base = (i * N_EXPERTS + eg * ng + k) * LANES + s * sub + u
            c0, c1 = cnt_ref[base], cnt_ref[base + 1]
            r0.append((c0 // 16) * 16)
            n_win = jnp.maximum(n_win, jnp.where(c1 > c0, (c1 - r0[k] + ROW_WINDOW - 1) // ROW_WINDOW, 0))
        hb = h_ref[u * tt:(u + 1) * tt, :]

        def window(w, carry, r0=r0, hb=hb, u=u):
            starts, pieces = [], []
            for k in range(ng):
                want = r0[k] + w * ROW_WINDOW
                rs = pl.multiple_of(jnp.minimum(want, last_start), 16)
                rows = rs + lax.broadcasted_iota(jnp.int32, (ROW_WINDOW, tt), 0)
                hit = (pos_ref[k, :, u * tt:(u + 1) * tt] == rows) & (rows >= want)
                pieces.append(jnp.where(hit, 1.0, 0.0).astype(BF16))
                starts.append(rs)
            got = jnp.dot(jnp.concatenate(pieces, axis=0), hb, preferred_element_type=F32)
            for k in range(ng):
                o_ref[k, pl.ds(starts[k], ROW_WINDOW), :] += got[k * ROW_WINDOW:(k + 1) * ROW_WINDOW].astype(BF16)
            return carry

        lax.fori_loop(0, n_win, window, 0)


def _gather_tokens_per_step(nt):
    for k in (5, 4, 2, 1):
        if nt % k == 0:
            return k * TOKEN_TILE


def _gather(cnt_flat, pos4, h2, cap):
    b, lt, d = h2.shape
    gt = _gather_tokens_per_step(lt // TOKEN_TILE)
    ng = GATHER_EXPERT_GROUP
    grid_spec = pltpu.PrefetchScalarGridSpec(
        num_scalar_prefetch=1,
        grid=(b, N_EXPERTS // ng, lt // gt),
        in_specs=[pl.BlockSpec((None, ng, 1, gt), lambda i, e, s, cnt: (i, e, 0, s)),
                  pl.BlockSpec((None, gt, d), lambda i, e, s, cnt: (i, s, 0))],
        out_specs=pl.BlockSpec((None, ng, cap, d), lambda i, e, s, cnt: (i, e, 0, 0)),
    )
    return pl.pallas_call(
        functools.partial(_gather_kernel, cap=cap),
        out_shape=jax.ShapeDtypeStruct((b, N_EXPERTS, cap, d), BF16),
        grid_spec=grid_spec,
        compiler_params=_cparams(("parallel", "parallel", "arbitrary"), VMEM_LIMIT),
        name="moe_gather",
    )(cnt_flat, pos4, h2)


def _ffn_kernel(x_ref, wg_ref, wu_ref, wd_ref, y_ref):
    x = x_ref[...]
    acc = jnp.zeros((x.shape[0], wd_ref.shape[1]), F32)
    for f in range(wg_ref.shape[1] // FFN_SLICE):
        sl = slice(f * FFN_SLICE, (f + 1) * FFN_SLICE)
        g = jnp.dot(x, wg_ref[:, sl], preferred_element_type=F32)
        u = jnp.dot(x, wu_ref[:, sl], preferred_element_type=F32)
        a = (g * (1.0 / (1.0 + jnp.exp(-g))) * u).astype(BF16)
        acc = acc + jnp.dot(a, wd_ref[sl, :], preferred_element_type=F32)
    y_ref[...] = acc.astype(BF16)


def _ffn(x_sel, wg, wu, wd, layer):
    b, ne, cap, d = x_sel.shape
    f = wg.shape[3]
    n_row_tiles = 2 if cap % 32 == 0 else 1
    rt = cap // n_row_tiles
    return pl.pallas_call(
        _ffn_kernel,
        out_shape=jax.ShapeDtypeStruct((b, ne, cap, d), BF16),
        grid=(ne, b * n_row_tiles),
        in_specs=[
            pl.BlockSpec((None, None, rt, d), lambda e, r: (r // n_row_tiles, e, r % n_row_tiles, 0)),
            pl.BlockSpec((None, None, d, f), lambda e, r: (layer, e, 0, 0)),
            pl.BlockSpec((None, None, d, f), lambda e, r: (layer, e, 0, 0)),
            pl.BlockSpec((None, None, f, d), lambda e, r: (layer, e, 0, 0)),
        ],
        out_specs=pl.BlockSpec((None, None, rt, d), lambda e, r: (r // n_row_tiles, e, r % n_row_tiles, 0)),
        compiler_params=_cparams(("parallel", "arbitrary"), VMEM_LIMIT),
        name="moe_ffn",
    )(x_sel, wg, wu, wd)


def _combine_kernel(cnt_ref, x_ref, pos_ref, aff_ref, mod_ref, *rest, cap, final):
    if final:
        gfin_ref, y_hbm, o_ref, buf_ref, xbuf_ref, acc_ref, sem_ref, xsem_ref = rest
    else:
        y_hbm, o_ref, buf_ref, xbuf_ref, acc_ref, sem_ref, xsem_ref = rest
    tt = TOKEN_TILE
    i, t = pl.program_id(0), pl.program_id(1)
    n_t = pl.num_programs(1)
    ne = pos_ref.shape[0]
    last_start = cap - ROW_WINDOW
    step = i * n_t + t
    slot = step % 2

    def first_window(bi, ti, e):
        c0 = cnt_ref[(bi * ne + e) * LANES + ti]
        return pl.multiple_of(jnp.minimum((c0 // 16) * 16, last_start), 16)

    def window_copy(bi, ti, e, sl):
        rs = first_window(bi, ti, e)
        return pltpu.make_async_copy(y_hbm.at[bi, e, pl.ds(rs, ROW_WINDOW), :], buf_ref.at[sl, e], sem_ref.at[sl, e])

    @pl.when(step == 0)
    def _():
        for e in range(ne):
            window_copy(i, t, e, slot).start()

    @pl.when(step + 1 < pl.num_programs(0) * n_t)
    def _():
        wrap = t + 1 == n_t
        for e in range(ne):
            window_copy(jnp.where(wrap, i + 1, i), jnp.where(wrap, 0, t + 1), e, 1 - slot).start()

    pos_f = pos_ref[...].astype(F32)
    pad = jnp.full((LANES - ne, tt), -1.0, F32)
    posT = jnp.concatenate([pos_f, pad], axis=0).T
    lane = lax.broadcasted_iota(jnp.int32, (1, ROW_WINDOW), 1).astype(F32)

    gated = []
    for e in range(ne):
        hit = posT[:, e:e + 1] == first_window(i, t, e).astype(F32) + lane
        gated.append(jnp.where(hit, aff_ref[:, e:e + 1], 0.0).astype(BF16))
    for e in range(ne):
        window_copy(i, t, e, slot).wait()
    rows = buf_ref[slot].reshape(ne * ROW_WINDOW, buf_ref.shape[-1])
    acc_ref[...] = jnp.dot(jnp.concatenate(gated, axis=1), rows, preferred_element_type=F32)

    for e in range(ne):
        rs = first_window(i, t, e)
        c1 = cnt_ref[(i * ne + e) * LANES + t + 1]
        pcol = posT[:, e:e + 1]
        gate = aff_ref[:, e:e + 1]
        n_extra = jnp.maximum(c1 - rs - 1, 0) // ROW_WINDOW

        def extra(w, carry, e=e, rs=rs, pcol=pcol, gate=gate):
            want = rs + (w + 1) * ROW_WINDOW
            start = pl.multiple_of(jnp.minimum(want, last_start), 16)
            cp = pltpu.make_async_copy(y_hbm.at[i, e, pl.ds(start, ROW_WINDOW), :], xbuf_ref, xsem_ref)
            cp.start()
            cp.wait()
            row = start.astype(F32) + lane
            hit = (pcol == row) & (row >= want.astype(F32))
            acc_ref[...] += gate * jnp.dot(jnp.where(hit, 1.0, 0.0).astype(BF16), xbuf_ref[...],
                                           preferred_element_type=F32)
            return carry

        lax.fori_loop(0, n_extra, extra, 0)

    x2 = x_ref[...] + mod_ref[5:6, :] * acc_ref[...]
    if final:
        ms = jnp.mean(x2 * x2, axis=-1, keepdims=True)
        x2 = x2 * lax.rsqrt(ms + NORM_EPS) * gfin_ref[...]
    o_ref[...] = x2


def _combine(cnt_flat, x1, pos, aff, modsel, y, n_lat, final_g=None):
    b, lt, d = x1.shape
    tt = TOKEN_TILE
    nt = lt // tt
    cap = y.shape[2]
    final = final_g is not None
    n_steps = n_lat // tt if final else nt
    n_out = n_lat if final else lt
    in_specs = [
        pl.BlockSpec((None, tt, d), lambda i, t, cnt: (i, t, 0)),
        pl.BlockSpec((None, N_EXPERTS, tt), lambda i, t, cnt: (i, 0, t)),
        pl.BlockSpec((None, tt, LANES), lambda i, t, cnt: (i, t, 0)),
        pl.BlockSpec((None, None, N_MOD, d), lambda i, t, cnt: (i, jnp.where(t == nt - 1, 0, 1), 0, 0)),
    ]
    args = [x1, pos, aff, modsel]
    if final:
        in_specs.append(pl.BlockSpec((1, d), lambda i, t, cnt: (0, 0)))
        args.append(final_g)
    in_specs.append(pl.BlockSpec(memory_space=pl.ANY))
    args.append(y)
    grid_spec = pltpu.PrefetchScalarGridSpec(
        num_scalar_prefetch=1,
        grid=(b, n_steps),
        in_specs=in_specs,
        out_specs=pl.BlockSpec((None, tt, d), lambda i, t, cnt: (i, t, 0)),
        scratch_shapes=[
            pltpu.VMEM((2, N_EXPERTS, ROW_WINDOW, d), BF16),
            pltpu.VMEM((ROW_WINDOW, d), BF16),
            pltpu.VMEM((tt, d), F32),
            pltpu.SemaphoreType.DMA((2, N_EXPERTS)),
            pltpu.SemaphoreType.DMA(()),
        ],
    )
    return pl.pallas_call(
        functools.partial(_combine_kernel, cap=cap, final=final),
        out_shape=jax.ShapeDtypeStruct((b, n_out, d), F32),
        grid_spec=grid_spec,
        compiler_params=_cparams(("arbitrary", "arbitrary"), VMEM_LIMIT),
        name="moe_combine",
    )(cnt_flat, *args)


def _rope_tables(n_lat, n_ctx):
    rows = n_lat // GRID_W
    row = jnp.repeat(jnp.arange(rows, dtype=F32), GRID_W)
    col = jnp.tile(jnp.arange(GRID_W, dtype=F32), rows)
    inv_freq = 1.0 / (ROPE_THETA ** (jnp.arange(0, ROPE_AXIS_DIM, 2, dtype=F32) / ROPE_AXIS_DIM))
    ang_r = (row[:, None] * inv_freq).T
    ang_c = (col[:, None] * inv_freq).T
    ct = jnp.concatenate([jnp.cos(ang_r), jnp.cos(ang_r), jnp.cos(ang_c), jnp.cos(ang_c)], axis=0)
    st = jnp.concatenate([-jnp.sin(ang_r), jnp.sin(ang_r), -jnp.sin(ang_c), jnp.sin(ang_c)], axis=0)
    ct = jnp.concatenate([ct, jnp.ones((HEAD_DIM, n_ctx), F32)], axis=1)
    st = jnp.concatenate([st, jnp.zeros((HEAD_DIM, n_ctx), F32)], axis=1)
    return ct, st


def kernel(x, c, ctx, c_ctx, w_mod, b_mod, norm_mix_g, norm_ffn_g, w_qkv, w_o, q_norm_g, k_norm_g,
           attn_sink, w_router, w_gate, w_up, w_down, final_norm_g):
    b, n_lat, d = x.shape
    n_ctx = ctx.shape[1]
    depth = w_mod.shape[0]
    tt = TOKEN_TILE
    assert n_ctx == tt and n_lat % tt == 0 and d == N_Q_HEADS * HEAD_DIM
    cap_lat = max(1, EC_CAPACITY * n_lat // N_EXPERTS)
    cap_ctx = max(1, EC_CAPACITY * n_ctx // N_EXPERTS)
    cap = cap_lat + cap_ctx
    assert cap % 16 == 0 and cap >= ROW_WINDOW and (cap - ROW_WINDOW) % 16 == 0

    cond = jnp.concatenate([c, c_ctx[None, :], jnp.zeros((8 - b - 1, d), F32)], axis=0)
    mod = _modulation(cond, w_mod, b_mod)
    mod = mod.reshape(depth, 8, N_MOD, d)
    modsel = jnp.stack([jnp.broadcast_to(mod[:, b][:, None], (depth, b, N_MOD, d)), mod[:, :b]], axis=2)

    ct, st = _rope_tables(n_lat, n_ctx)
    xs = jnp.concatenate([x, ctx], axis=1)
    n_qk = (N_Q_HEADS + N_KV_HEADS) * HEAD_DIM
    f = w_gate.shape[3]
    ffn_w, ffn_base = None, 0

    for i in range(depth):
        last = i == depth - 1
        is_global = i % 2 == 0
        j = i // 2
        wT = w_qkv[i].T.astype(BF16)
        if is_global:
            gqk = jnp.concatenate([jnp.tile(q_norm_g[j], N_Q_HEADS), jnp.tile(k_norm_g[j], N_KV_HEADS)])
        else:
            gqk = jnp.ones((n_qk,), F32)
        gqk = jnp.broadcast_to(gqk[:, None], (n_qk, tt))
        qT, k, vT = _pre_attn(xs, modsel[i], norm_mix_g[i][None, :], wT, ct, st, gqk, is_global)
        if is_global:
            n_span = min(i + 2, depth) - i
            rows_up, rows_down = N_EXPERTS * d, N_EXPERTS * f
            o_lat, cast = _attn_global(qT, k, vT, n_lat, [
                (w_gate.reshape(-1, f), i * rows_up, n_span * rows_up),
                (w_up.reshape(-1, f), i * rows_up, n_span * rows_up),
                (w_down.reshape(-1, d), i * rows_down, n_span * rows_down)])
            ffn_w = (cast[0].reshape(n_span, N_EXPERTS, d, f), cast[1].reshape(n_span, N_EXPERTS, d, f),
                     cast[2].reshape(n_span, N_EXPERTS, f, d))
            ffn_base = i
            o_ctx = _attn_ctx(qT, k, vT, None, n_lat)
        else:
            sink = jnp.repeat(attn_sink[j].reshape(N_KV_HEADS, GQA_GROUP) * LOG2_E, tt, axis=1)[:, None, :]
            o_lat = _attn_window(qT, k, vT, sink, n_lat)
            o_ctx = _attn_ctx(qT, k, vT, sink, n_lat)
        wr = jnp.pad(w_router[i], ((0, 0), (0, LANES - N_EXPERTS)))
        wr_hi = wr.astype(BF16)
        wr = jnp.stack([wr_hi, (wr - wr_hi.astype(F32)).astype(BF16)])
        x1, h2, aff, affT = _post_attn(o_lat.reshape(b, d, n_lat), o_ctx.reshape(b, d, n_ctx), xs, modsel[i],
                                       norm_ffn_g[i][None, :], w_o[i].T.astype(BF16), wr)
        pos, cnt = _route(affT, n_lat, cap_lat, cap_ctx)
        cnt_flat = cnt.reshape(-1)
        x_sel = _gather(cnt_flat, pos.reshape(b, N_EXPERTS, 1, n_lat + n_ctx), h2, cap)
        y = _ffn(x_sel, *ffn_w, i - ffn_base)
        xs = _combine(cnt_flat, x1, pos, aff, modsel[i], y, n_lat,
                      final_g=final_norm_g[None, :] if last else None)
    return xs
```
